```python
import math
import jax
import jax.numpy as jnp
from jax import lax
import numpy as np

D_MODEL = 1024
BATCH = 8
SEQ = 4096
DEPTH = 2
DEC_BATCH = 32
DEC_SEQ = 8
PAST_LEN = 16384
PAGE_SIZE = 128

N_A = DEPTH // 2
N_B = DEPTH - N_A
H_QK = D_MODEL // 128
H_V = 2 * H_QK
DK_A = 128
DV_A = 128
CONV_W = 4
CHUNK = 64
QK_DIM = H_QK * DK_A
VDIM_A = H_V * DV_A
CONV_DIM = 2 * QK_DIM + VDIM_A
IN_A = CONV_DIM + VDIM_A + 2 * H_V
H_B = D_MODEL // 128
DK_B = 64
DV_B = 2 * DK_B
Q_DIM_B = H_B * 2 * DK_B
KV_DIM = H_B * 2 * DK_B + H_B * DV_B
QBLK = 128
NUM_BUCKETS = 32
MAX_DISTANCE = 128
D_FF = 4 * D_MODEL
PLE_DIM = 256
ALPHA = (2 * DEPTH) ** 0.25
BETA = (8 * DEPTH) ** -0.25

kernel_name = 'yoco_gdn_diffattn_step'

F32 = jnp.float32


def _layernorm(x, g, b, eps=1e-5):
    xf = x.astype(F32)
    mu = jnp.mean(xf, -1, keepdims=True)
    xc = xf - mu
    var = jnp.mean(xc * xc, -1, keepdims=True)
    return (xc * lax.rsqrt(var + eps) * g + b).astype(x.dtype)


def _rmsnorm(x, w, eps=1e-6):
    xf = x.astype(F32)
    return (xf * lax.rsqrt(jnp.mean(xf * xf, -1, keepdims=True) + eps) * w).astype(x.dtype)


def _l2norm(x, eps=1e-6):
    xf = x.astype(F32)
    return (xf * lax.rsqrt(jnp.sum(xf * xf, -1, keepdims=True) + eps)).astype(x.dtype)


def _gated_delta_rule(q, k, v, g, beta, s0):
    out_dtype, st_dtype = v.dtype, s0.dtype
    q, k, v, g, beta, s = (a.astype(F32) for a in (q, k, v, g, beta, s0))
    B, T, H, DK = q.shape
    DV = v.shape[-1]
    C = min(CHUNK, T)
    n = -(-T // C)
    pad = n * C - T
    if pad:
        pw = ((0, 0), (0, pad), (0, 0), (0, 0))
        q, k, v = jnp.pad(q, pw), jnp.pad(k, pw), jnp.pad(v, pw)
        g, beta = jnp.pad(g, pw[:3]), jnp.pad(beta, pw[:3])
    q = q.reshape(B, n, C, H, DK).transpose(1, 0, 3, 2, 4)
    k = k.reshape(B, n, C, H, DK).transpose(1, 0, 3, 2, 4)
    v = v.reshape(B, n, C, H, DV).transpose(1, 0, 3, 2, 4)
    g = g.reshape(B, n, C, H).transpose(1, 0, 3, 2)
    beta = beta.reshape(B, n, C, H).transpose(1, 0, 3, 2)
    gc = jnp.cumsum(g, axis=-1)
    idx = jnp.arange(C)
    incl = idx[:, None] >= idx[None, :]
    strict = idx[:, None] > idx[None, :]
    dmat = jnp.exp(jnp.where(incl, gc[..., :, None] - gc[..., None, :], -jnp.inf))
    kb = k * beta[..., None]
    a_kk = jnp.where(strict, jnp.einsum('nbhcd,nbhed->nbhce', kb, k) * dmat, 0.0)
    rhs = jnp.concatenate([v * beta[..., None], kb * jnp.exp(gc)[..., None]], axis=-1)
    uw = lax.linalg.triangular_solve(a_kk + jnp.eye(C, dtype=F32), rhs, left_side=True, lower=True, unit_diagonal=True)
    u, w = uw[..., :DV], uw[..., DV:]
    a_qk = jnp.einsum('nbhcd,nbhed->nbhce', q, k) * dmat

    def step(s, inp):
        q_c, k_c, u_c, w_c, g_c, a_c = inp
        v_new = u_c - jnp.einsum('bhcd,bhde->bhce', w_c, s)
        o = jnp.einsum('bhcd,bhde->bhce', q_c * jnp.exp(g_c)[..., None], s) + jnp.einsum('bhce,bhef->bhcf', a_c, v_new)
        g_last = g_c[..., -1:]
        s = s * jnp.exp(g_last)[..., None] + jnp.einsum('bhcd,bhce->bhde', k_c * jnp.exp(g_last - g_c)[..., None], v_new)
        return s, o

    s, o = lax.scan(step, s, (q, k, u, w, gc, a_qk))
    o = o.transpose(1, 0, 3, 2, 4).reshape(B, n * C, H, DV)[:, :T]
    return o.astype(out_dtype), s.astype(st_dtype)


def _gdn_mixer(x, w_in, conv_w, a_log, dt_bias, gnorm_w, w_out, conv_state, ssm_state):
    B, T, _ = x.shape
    proj = x @ w_in
    qkv = proj[..., :CONV_DIM]
    z = proj[..., CONV_DIM:CONV_DIM + VDIM_A]
    b = proj[..., CONV_DIM + VDIM_A:CONV_DIM + VDIM_A + H_V]
    a = proj[..., CONV_DIM + VDIM_A + H_V:]
    xpad = jnp.concatenate([conv_state.astype(qkv.dtype), qkv], axis=1)
    new_conv_state = xpad[:, -(CONV_W - 1):]
    c = lax.conv_general_dilated(xpad, conv_w[:, None, :].astype(xpad.dtype), (1,), 'VALID',
                                 dimension_numbers=('NWC', 'WIO', 'NWC'), feature_group_count=CONV_DIM)
    c = jax.nn.silu(c)
    q = _l2norm(c[..., :QK_DIM].reshape(B, T, H_QK, DK_A)) * (DK_A ** -0.5)
    k = _l2norm(c[..., QK_DIM:2 * QK_DIM].reshape(B, T, H_QK, DK_A))
    v = c[..., 2 * QK_DIM:].reshape(B, T, H_V, DV_A)
    q = jnp.repeat(q, H_V // H_QK, axis=2)
    k = jnp.repeat(k, H_V // H_QK, axis=2)
    beta = jax.nn.sigmoid(b.astype(F32))
    g = -jnp.exp(a_log.astype(F32)) * jax.nn.softplus(a.astype(F32) + dt_bias.astype(F32))
    o, new_ssm = _gated_delta_rule(q, k, v, g, beta, ssm_state)
    o = _rmsnorm(o, gnorm_w) * jax.nn.silu(z.reshape(B, T, H_V, DV_A))
    return o.reshape(B, T, VDIM_A).astype(x.dtype) @ w_out, new_conv_state, new_ssm


def _t5_bucket(n):
    max_exact = NUM_BUCKETS // 2
    nf = jnp.maximum(n, max_exact).astype(F32)
    large = max_exact + (jnp.log(nf / max_exact) / math.log(MAX_DISTANCE / max_exact) * (NUM_BUCKETS - max_exact)).astype(jnp.int32)
    large = jnp.minimum(large, NUM_BUCKETS - 1)
    return jnp.where(n < max_exact, n, large)


def _rel_bias(qpos, kpos, table):
    n = jnp.maximum(qpos[:, None] - kpos[None, :], 0)
    return table[_t5_bucket(n)].transpose(2, 0, 1).astype(F32)


def _diff_block_stats(q, k, v, bias, mask):
    s = jnp.einsum('bqhid,bkhid->bhiqk', q, k, preferred_element_type=F32) + bias[None, :, None]
    if mask is not None:
        s = jnp.where(mask, s, -jnp.inf)
    m = jnp.max(s, -1)
    p = jnp.exp(s - m[..., None])
    return m, jnp.sum(p, -1), jnp.einsum('bhiqk,bkhd->bhiqd', p, v.astype(F32))


def _merge(a, b):
    m = jnp.maximum(a[0], b[0])
    ca = jnp.exp(a[0] - m)
    cb = jnp.exp(b[0] - m)
    return (m, a[1] * ca + b[1] * cb, a[2] * ca[..., None] + b[2] * cb[..., None])


def _diff_attn_prompt(q, k, v, rel_bias):
    B, T = q.shape[0], q.shape[1]
    qb = min(QBLK, T)
    nb = T // qb
    kpos = jnp.arange(T)

    def block(bi):
        q0 = bi * qb
        q_blk = lax.dynamic_slice_in_dim(q, q0, qb, axis=1)
        qpos = q0 + jnp.arange(qb)
        mask = kpos[None, :] <= qpos[:, None]
        m, l, acc = _diff_block_stats(q_blk, k, v, _rel_bias(qpos, kpos, rel_bias), mask)
        return acc / l[..., None]

    w = lax.map(block, jnp.arange(nb))
    return w.transpose(1, 2, 3, 0, 4, 5).reshape(B, H_B, 2, T, DV_B)


def _diff_attn_sample(q, k_new, v_new, cache_k, cache_v, page_table, rel_bias):
    Bd, T = q.shape[0], q.shape[1]
    n_pages = page_table.shape[1]
    qpos = n_pages * PAGE_SIZE + jnp.arange(T)
    init = (jnp.full((Bd, H_B, 2, T), -jnp.inf, F32), jnp.zeros((Bd, H_B, 2, T), F32),
            jnp.zeros((Bd, H_B, 2, T, DV_B), F32))

    def step(carry, inp):
        j, phys = inp
        kp = cache_k[phys].reshape(Bd, PAGE_SIZE, H_B, 2, DK_B)
        vp = cache_v[phys]
        kpos = j * PAGE_SIZE + jnp.arange(PAGE_SIZE)
        return _merge(carry, _diff_block_stats(q, kp, vp, _rel_bias(qpos, kpos, rel_bias), None)), None

    carry, _ = lax.scan(step, init, (jnp.arange(n_pages), page_table.T))
    own = _diff_block_stats(q, k_new, v_new, _rel_bias(qpos, qpos, rel_bias), qpos[None, :] <= qpos[:, None])
    m, l, acc = _merge(carry, own)
    return acc / l[..., None]


def _diff_out(w, lam, lam_init, subln_w, w_o):
    o = w[:, :, 0] - lam * w[:, :, 1]
    o = _rmsnorm(o, subln_w, 1e-5) * (1.0 - lam_init)
    B, H, T, DV = o.shape
    return o.transpose(0, 2, 1, 3).reshape(B, T, H * DV).astype(w_o.dtype) @ w_o


def setup_inputs(seed: int = 0) -> dict:
    key = jax.random.key(seed)
    ks = iter(jax.random.split(key, 40))

    def nrm(shape, scale):
        return jax.random.normal(next(ks), shape, F32) * scale

    n_pages = PAST_LEN // PAGE_SIZE
    n_pool = (5 * DEC_BATCH * n_pages + 3) // 4
    x_prompt = nrm((BATCH, SEQ, D_MODEL), 1.0)
    x_sample = nrm((DEC_BATCH, DEC_SEQ, D_MODEL), 1.0)
    state_conv = nrm((N_A, DEC_BATCH, CONV_W - 1, CONV_DIM), 1.0)
    state_ssm = nrm((N_A, DEC_BATCH, H_V, DK_A, DV_A), 0.1)
    cache_k = nrm((n_pool, PAGE_SIZE, H_B, 2 * DK_B), 1.0)
    cache_v = nrm((n_pool, PAGE_SIZE, H_B, DV_B), 1.0)
    page_table = jax.random.permutation(next(ks), n_pool)[:DEC_BATCH * n_pages].reshape(DEC_BATCH, n_pages).astype(jnp.int32)
    p_prompt = nrm((DEPTH, BATCH, SEQ, PLE_DIM), 1.0)
    p_sample = nrm((DEPTH, DEC_BATCH, DEC_SEQ, PLE_DIM), 1.0)
    w_in_a = nrm((N_A, D_MODEL, IN_A), D_MODEL ** -0.5)
    conv_w_a = nrm((N_A, CONV_W, CONV_DIM), CONV_W ** -0.5)
    a_log_a = jnp.log(jax.random.uniform(next(ks), (N_A, H_V), F32, 1.0, 16.0))
    dt = jnp.exp(jax.random.uniform(next(ks), (N_A, H_V), F32, math.log(1e-3), math.log(1e-1)))
    dt_bias_a = dt + jnp.log(-jnp.expm1(-dt))
    gnorm_w_a = 1.0 + nrm((N_A, DV_A), 0.02)
    w_out_a = nrm((N_A, VDIM_A, D_MODEL), BETA * VDIM_A ** -0.5)
    w_kv = nrm((D_MODEL, KV_DIM), D_MODEL ** -0.5)
    w_q_b = nrm((N_B, D_MODEL, Q_DIM_B), D_MODEL ** -0.5)
    lam_b = nrm((N_B, 4, DK_B), 0.1)
    subln_w_b = 1.0 + nrm((N_B, DV_B), 0.02)
    w_o_b = nrm((N_B, H_B * DV_B, D_MODEL), BETA * (H_B * DV_B) ** -0.5)
    rel_bias = nrm((NUM_BUCKETS, H_B), 0.5)
    ln_g = 1.0 + nrm((DEPTH, 2, D_MODEL), 0.02)
    ln_b = nrm((DEPTH, 2, D_MODEL), 0.02)
    w_up = nrm((DEPTH, D_MODEL, D_FF), D_MODEL ** -0.5)
    w_down = nrm((DEPTH, D_FF, D_MODEL), BETA * D_FF ** -0.5)
    w_ple = nrm((DEPTH, PLE_DIM, D_MODEL), PLE_DIM ** -0.5)
    w_pg = nrm((DEPTH, D_MODEL, D_MODEL), D_MODEL ** -0.5)
    return {'x_prompt': x_prompt, 'x_sample': x_sample, 'state_conv': state_conv, 'state_ssm': state_ssm,
            'cache_k': cache_k, 'cache_v': cache_v, 'page_table': page_table, 'p_prompt': p_prompt,
            'p_sample': p_sample, 'w_in_a': w_in_a, 'conv_w_a': conv_w_a, 'a_log_a': a_log_a,
            'dt_bias_a': dt_bias_a, 'gnorm_w_a': gnorm_w_a, 'w_out_a': w_out_a, 'w_kv': w_kv,
            'w_q_b': w_q_b, 'lam_b': lam_b, 'subln_w_b': subln_w_b, 'w_o_b': w_o_b, 'rel_bias': rel_bias,
            'ln_g': ln_g, 'ln_b': ln_b, 'w_up': w_up, 'w_down': w_down, 'w_ple': w_ple, 'w_pg': w_pg}


def reference(x_prompt, x_sample, state_conv, state_ssm, cache_k, cache_v, page_table, p_prompt, p_sample,
              w_in_a, conv_w_a, a_log_a, dt_bias_a, gnorm_w_a, w_out_a, w_kv, w_q_b, lam_b, subln_w_b, w_o_b,
              rel_bias, ln_g, ln_b, w_up, w_down, w_ple, w_pg):
    def run(x, p, conv_state, ssm_state, attn_fn):
        Bn, T, _ = x.shape
        convs, ssms = [], []
        k_sh = v_sh = None
        for i in range(DEPTH):
            if i < N_A:
                h, cs, ss = _gdn_mixer(x, w_in_a[i], conv_w_a[i], a_log_a[i], dt_bias_a[i], gnorm_w_a[i],
                                       w_out_a[i], conv_state[i], ssm_state[i])
                convs.append(cs)
                ssms.append(ss)
            else:
                j = i - N_A
                lam_init = 0.8 - 0.6 * math.exp(-0.3 * i)
                lq = lam_b[j].astype(F32)
                lam = jnp.exp(jnp.sum(lq[0] * lq[1])) - jnp.exp(jnp.sum(lq[2] * lq[3])) + lam_init
                q = (x @ w_q_b[j]).reshape(Bn, T, H_B, 2, DK_B) * (DK_B ** -0.5)
                h = _diff_out(attn_fn(q, k_sh, v_sh), lam, lam_init, subln_w_b[j], w_o_b[j])
            x = _layernorm(ALPHA * x + h, ln_g[i, 0], ln_b[i, 0])
            u = jax.nn.relu(x @ w_up[i])
            x = _layernorm(ALPHA * x + (u * u) @ w_down[i], ln_g[i, 1], ln_b[i, 1])
            x = x + (p[i] @ w_ple[i]) * jax.nn.sigmoid(x @ w_pg[i])
            if i == N_A - 1:
                kv = x @ w_kv
                k_sh = kv[..., :H_B * 2 * DK_B].reshape(Bn, T, H_B, 2, DK_B)
                v_sh = kv[..., H_B * 2 * DK_B:].reshape(Bn, T, H_B, DV_B)
        return x, jnp.stack(convs), jnp.stack(ssms), k_sh.reshape(Bn, T, H_B, 2 * DK_B), v_sh

    Bp = x_prompt.shape[0]
    conv0 = jnp.zeros((N_A, Bp, CONV_W - 1, CONV_DIM), x_prompt.dtype)
    ssm0 = jnp.zeros((N_A, Bp, H_V, DK_A, DV_A), state_ssm.dtype)
    y_prompt, conv_p, ssm_p, k_p, v_p = run(
        x_prompt, p_prompt, conv0, ssm0,
        lambda q, k, v: _diff_attn_prompt(q, k, v, rel_bias))
    y_sample, conv_s, ssm_s, k_s, v_s = run(
        x_sample, p_sample, state_conv, state_ssm,
        lambda q, k, v: _diff_attn_sample(q, k, v, cache_k, cache_v, page_table, rel_bias))
    return (y_prompt, y_sample, conv_p, ssm_p, k_p, v_p, conv_s, ssm_s, k_s, v_s)
```

```python
import functools
import math

import jax
import jax.numpy as jnp
from jax import lax
from jax.experimental import pallas as pl
from jax.experimental.pallas import tpu as pltpu

F32 = jnp.float32
BF16 = jnp.bfloat16

D_MODEL = 1024
DEPTH = 2
N_A = DEPTH // 2
PAGE_SIZE = 128
H_QK = 8
H_V = 16
DK_A = 128
DV_A = 128
CONV_W = 4
CHUNK = 64
QK_DIM = H_QK * DK_A
VDIM_A = H_V * DV_A
CONV_DIM = 2 * QK_DIM + VDIM_A
IN_A = CONV_DIM + VDIM_A + 2 * H_V
IN_A_PAD = CONV_DIM + VDIM_A + 128
H_B = 8
DK_B = 64
DV_B = 128
NUM_BUCKETS = 32
MAX_DISTANCE = 128
D_FF = 4 * D_MODEL
ALPHA = (2 * DEPTH) ** 0.25

LANES = 128
VMEM_LIMIT = 56 * 1024 * 1024
NEG_INF = float("-inf")


def _params(sem, vmem=VMEM_LIMIT):
    return pltpu.CompilerParams(dimension_semantics=sem, vmem_limit_bytes=vmem)


def _sigmoid(x):
    return 1.0 / (1.0 + jnp.exp(-x))


def _mm_body(x_ref, w_ref, o_ref, *, act):
    acc = jnp.dot(x_ref[...].astype(BF16), w_ref[...], preferred_element_type=F32)
    if act == "relu2":
        r = jnp.maximum(acc, 0.0)
        acc = r * r
    o_ref[...] = acc.astype(o_ref.dtype)


def _mm(x, w, *, act=None, out_dtype=F32, tm=1024, tn=None):
    M, K = x.shape
    N = w.shape[1]
    tm = min(tm, M)
    tn = N if tn is None else tn
    assert M % tm == 0 and N % tn == 0
    return pl.pallas_call(
        functools.partial(_mm_body, act=act),
        grid=(M // tm, N // tn),
        in_specs=[pl.BlockSpec((tm, K), lambda i, j: (i, 0)),
                  pl.BlockSpec((K, tn), lambda i, j: (0, j))],
        out_specs=pl.BlockSpec((tm, tn), lambda i, j: (i, j)),
        out_shape=jax.ShapeDtypeStruct((M, N), out_dtype),
        compiler_params=_params(("parallel", "arbitrary")),
        name="mm",
    )(x, w)


def _mm_ln_body(x_ref, w_ref, r_ref, g_ref, b_ref, o_ref):
    acc = jnp.dot(x_ref[...].astype(BF16), w_ref[...], preferred_element_type=F32)
    y = ALPHA * r_ref[...] + acc
    mu = jnp.mean(y, -1, keepdims=True)
    yc = y - mu
    var = jnp.mean(yc * yc, -1, keepdims=True)
    o_ref[...] = yc * lax.rsqrt(var + 1e-5) * g_ref[...] + b_ref[...]


def _mm_ln(x, w, res, g, b, *, tm=512):
    M, K = x.shape
    N = w.shape[1]
    tm = min(tm, M)
    return pl.pallas_call(
        _mm_ln_body,
        grid=(M // tm,),
        in_specs=[pl.BlockSpec((tm, K), lambda i: (i, 0)),
                  pl.BlockSpec((K, N), lambda i: (0, 0)),
                  pl.BlockSpec((tm, N), lambda i: (i, 0)),
                  pl.BlockSpec((1, N), lambda i: (0, 0)),
                  pl.BlockSpec((1, N), lambda i: (0, 0))],
        out_specs=pl.BlockSpec((tm, N), lambda i: (i, 0)),
        out_shape=jax.ShapeDtypeStruct((M, N), F32),
        compiler_params=_params(("parallel",)),
        name="mm_ln",
    )(x, w, res, g.reshape(1, N), b.reshape(1, N))


def _ple_gate_body(x_ref, p_ref, wple_ref, wpg_ref, o_ref):
    x = x_ref[...]
    e = jnp.dot(p_ref[...].astype(BF16), wple_ref[...], preferred_element_type=F32)
    gt = jnp.dot(x.astype(BF16), wpg_ref[...], preferred_element_type=F32)
    o_ref[...] = x + e * _sigmoid(gt)


def _ple_gate(x, p, w_ple, w_pg, *, tm=512):
    M, D = x.shape
    P = p.shape[1]
    tm = min(tm, M)
    return pl.pallas_call(
        _ple_gate_body,
        grid=(M // tm,),
        in_specs=[pl.BlockSpec((tm, D), lambda i: (i, 0)),
                  pl.BlockSpec((tm, P), lambda i: (i, 0)),
                  pl.BlockSpec((P, D), lambda i: (0, 0)),
                  pl.BlockSpec((D, D), lambda i: (0, 0))],
        out_specs=pl.BlockSpec((tm, D), lambda i: (i, 0)),
        out_shape=jax.ShapeDtypeStruct((M, D), F32),
        compiler_params=_params(("parallel",)),
        name="ple_gate",
    )(x, p, w_ple, w_pg)


def _gdn_prep_body(x_ref, halo_ref, st_ref, cw_ref, o_ref, *, gw):
    t = pl.program_id(1)
    j = pl.program_id(2)
    prev = jnp.where(t == 0, st_ref[0], halo_ref[0])
    xf = jnp.concatenate([prev, x_ref[0]], axis=0)
    for h in range(gw // LANES):
        sl = slice(h * LANES, (h + 1) * LANES)
        xs = xf[:, sl]
        cw = cw_ref[:, sl]
        c = xs * cw[3:4]
        for s in (1, 2, 3):
            c = c + pltpu.roll(xs, s, 0) * cw[3 - s:4 - s]
        c = c[8:]
        c = c * _sigmoid(c)
        group = j * (gw // LANES) + h
        nrm = lax.rsqrt(jnp.sum(c * c, -1, keepdims=True) + 1e-6)
        scale = jnp.where(group < H_QK, nrm * (DK_A ** -0.5), jnp.where(group < 2 * H_QK, nrm, 1.0))
        o_ref[0, :, sl] = c * scale


def _gdn_prep(proj, state8, conv_w, *, tt, gw=512):
    B, T, _ = proj.shape
    tt = min(tt, T)
    hb = tt // 8
    return pl.pallas_call(
        functools.partial(_gdn_prep_body, gw=gw),
        grid=(B, T // tt, CONV_DIM // gw),
        in_specs=[pl.BlockSpec((1, tt, gw), lambda b, t, j: (b, t, j)),
                  pl.BlockSpec((1, 8, gw), lambda b, t, j: (b, jnp.maximum(t * hb - 1, 0), j)),
                  pl.BlockSpec((1, 8, gw), lambda b, t, j: (b, 0, j)),
                  pl.BlockSpec((CONV_W, gw), lambda b, t, j: (0, j))],
        out_specs=pl.BlockSpec((1, tt, gw), lambda b, t, j: (b, t, j)),
        out_shape=jax.ShapeDtypeStruct((B, T, CONV_DIM), F32),
        compiler_params=_params(("parallel", "parallel", "parallel")),
        name="gdn_prep",
    )(proj, proj, state8, conv_w)


def _gdn_gates_body(x_ref, alog_ref, dtb_ref, o_ref):
    ba = x_ref[...]
    lane = lax.broadcasted_iota(jnp.int32, ba.shape, 1)
    beta = _sigmoid(ba)
    xx = ba + dtb_ref[...]
    softplus = jnp.maximum(xx, 0.0) + jnp.log(1.0 + jnp.exp(-jnp.abs(xx)))
    g = -jnp.exp(alog_ref[...]) * softplus
    o_ref[...] = jnp.where(lane < H_V, beta, jnp.where(lane < 2 * H_V, g, 0.0))


def _gdn_gates(proj2d, a_log, dt_bias, *, tm=1024):
    M = proj2d.shape[0]
    tm = min(tm, M)
    pad = lambda v: jnp.pad(v.astype(F32), (H_V, LANES - 2 * H_V)).reshape(1, LANES)
    col = (CONV_DIM + VDIM_A) // LANES
    return pl.pallas_call(
        _gdn_gates_body,
        grid=(M // tm,),
        in_specs=[pl.BlockSpec((tm, LANES), lambda i: (i, col)),
                  pl.BlockSpec((1, LANES), lambda i: (0, 0)),
                  pl.BlockSpec((1, LANES), lambda i: (0, 0))],
        out_specs=pl.BlockSpec((tm, LANES), lambda i: (i, 0)),
        out_shape=jax.ShapeDtypeStruct((M, LANES), F32),
        compiler_params=_params(("parallel",)),
        name="gdn_gates",
    )(proj2d, pad(a_log), pad(dt_bias))


def _dot_nt(a, b):
    return lax.dot_general(a, b, (((1,), (1,)), ((), ())), preferred_element_type=F32)


def _dot_tn(a, b):
    return lax.dot_general(a, b, (((0,), (0,)), ((), ())), preferred_element_type=F32)


def _gdn_chunk_body(q_ref, k_ref, v_ref, z_ref, bg_ref, bgt_ref, s0_ref, gw_ref, y_ref, st_ref, s_scr):
    hk = pl.program_id(1)
    c = pl.program_id(2)
    C = CHUNK

    @pl.when(c == 0)
    def _():
        s_scr[...] = s0_ref[0]

    q = q_ref[0]
    k = k_ref[0]
    bg = bg_ref[0]
    bgt = bgt_ref[0, 0]
    row = lax.broadcasted_iota(jnp.int32, (C, C), 0)
    col = lax.broadcasted_iota(jnp.int32, (C, C), 1)
    incl = row >= col
    strict = row > col
    gc_cols = jnp.dot(incl.astype(F32), bg, precision=lax.Precision.HIGHEST, preferred_element_type=F32)
    gc_rows = jnp.dot(bgt, (row <= col).astype(F32), precision=lax.Precision.HIGHEST,
                      preferred_element_type=F32)
    lane = lax.broadcasted_iota(jnp.int32, (C, LANES), 1)
    sub = lax.broadcasted_iota(jnp.int32, (2 * H_V, C), 0)
    k16 = k.astype(BF16)
    kk = _dot_nt(k16, k16)
    qk = _dot_nt(q.astype(BF16), k16)
    eye = (row == col).astype(F32)

    for e in range(2):
        hv = 2 * hk + e
        beta = jnp.sum(jnp.where(lane == hv, bg, 0.0), axis=1, keepdims=True)
        gc = jnp.sum(jnp.where(lane == H_V + hv, gc_cols, 0.0), axis=1, keepdims=True)
        gc_r = jnp.sum(jnp.where(sub == H_V + hv, gc_rows, 0.0), axis=0, keepdims=True)
        dm = jnp.exp(jnp.where(incl, gc - gc_r, NEG_INF))
        npow = jnp.where(strict, -(beta * kk * dm), 0.0)
        tinv = eye + npow
        for _ in range(5):
            n16 = npow.astype(BF16)
            npow = jnp.dot(n16, n16, preferred_element_type=F32)
            tinv = tinv + jnp.dot(tinv.astype(BF16), npow.astype(BF16), preferred_element_type=F32)
        eg = jnp.exp(gc)
        v = v_ref[0, :, e * DV_A:(e + 1) * DV_A]
        rhs = jnp.concatenate([v * beta, k * (beta * eg)], axis=1).astype(BF16)
        uw = jnp.dot(tinv.astype(BF16), rhs, preferred_element_type=F32)
        u = uw[:, :DV_A]
        w = uw[:, DV_A:]
        s = s_scr[e]
        wq = jnp.concatenate([w, q * eg], axis=0).astype(BF16)
        r = jnp.dot(wq, s.astype(BF16), preferred_element_type=F32)
        v_new = (u - r[:C]).astype(BF16)
        aqk = jnp.where(incl, qk * dm, 0.0).astype(BF16)
        o = r[C:] + jnp.dot(aqk, v_new, preferred_element_type=F32)
        g_last = gc[C - 1:C, :]
        kd = (k * jnp.exp(g_last - gc)).astype(BF16)
        s_scr[e] = s * jnp.exp(g_last) + _dot_tn(kd, v_new)
        z = z_ref[0, :, e * DV_A:(e + 1) * DV_A]
        y = o * lax.rsqrt(jnp.mean(o * o, -1, keepdims=True) + 1e-6) * gw_ref[...] * (z * _sigmoid(z))
        y_ref[0, :, e * DV_A:(e + 1) * DV_A] = y.astype(y_ref.dtype)

    @pl.when(c == pl.num_programs(2) - 1)
    def _():
        st_ref[0] = s_scr[...]


def _gdn_chunks(qkvc, z_arr, z_col, bg, s0, gnorm_w):
    B, T, _ = qkvc.shape
    nc = T // CHUNK
    bgt = bg[..., :2 * H_V].reshape(B, nc, CHUNK, 2 * H_V).transpose(0, 1, 3, 2)
    kb = QK_DIM // DK_A
    return pl.pallas_call(
        _gdn_chunk_body,
        grid=(B, H_QK, nc),
        in_specs=[pl.BlockSpec((1, CHUNK, DK_A), lambda b, h, c: (b, c, h)),
                  pl.BlockSpec((1, CHUNK, DK_A), lambda b, h, c: (b, c, kb + h)),
                  pl.BlockSpec((1, CHUNK, 2 * DV_A), lambda b, h, c: (b, c, kb + h)),
                  pl.BlockSpec((1, CHUNK, 2 * DV_A), lambda b, h, c: (b, c, z_col + h)),
                  pl.BlockSpec((1, CHUNK, LANES), lambda b, h, c: (b, c, 0)),
                  pl.BlockSpec((1, 1, 2 * H_V, CHUNK), lambda b, h, c: (b, c, 0, 0)),
                  pl.BlockSpec((1, 2, DK_A, DV_A), lambda b, h, c: (b, h, 0, 0)),
                  pl.BlockSpec((1, DV_A), lambda b, h, c: (0, 0))],
        out_specs=[pl.BlockSpec((1, CHUNK, 2 * DV_A), lambda b, h, c: (b, c, h)),
                   pl.BlockSpec((1, 2, DK_A, DV_A), lambda b, h, c: (b, h, 0, 0))],
        out_shape=[jax.ShapeDtypeStruct((B, T, VDIM_A), BF16),
                   jax.ShapeDtypeStruct((B, H_V, DK_A, DV_A), F32)],
        scratch_shapes=[pltpu.VMEM((2, DK_A, DV_A), F32)],
        compiler_params=_params(("parallel", "parallel", "arbitrary")),
        name="gdn_chunks",
    )(qkvc, qkvc, qkvc, z_arr, bg, bgt, s0, gnorm_w.reshape(1, DV_A).astype(F32))


def _bucket_starts():
    max_exact = NUM_BUCKETS // 2
    starts = list(range(max_exact + 1))
    n = max_exact
    for b in range(max_exact + 1, NUM_BUCKETS):
        while max_exact + int(math.log(n / max_exact) / math.log(MAX_DISTANCE / max_exact)
                              * (NUM_BUCKETS - max_exact)) < b:
            n += 1
        starts.append(n)
    return starts


_BUCKET_START = _bucket_starts()


def _bias_of_distance(n, tab_ref, h):
    val = jnp.full(n.shape, tab_ref[(NUM_BUCKETS - 1) * H_B + h], F32)
    for b in range(NUM_BUCKETS - 2, -1, -1):
        val = jnp.where(n < _BUCKET_START[b + 1], tab_ref[b * H_B + h], val)
    return val


def _prompt_bias_body(tab_ref, o_ref, *, blk):
    h = pl.program_id(0)
    r = lax.broadcasted_iota(jnp.int32, (blk, blk), 0)
    c = lax.broadcasted_iota(jnp.int32, (blk, blk), 1)
    d = r - c
    o_ref[0, 0] = jnp.where(d >= 0, _bias_of_distance(jnp.maximum(d, 0), tab_ref, h), NEG_INF)
    o_ref[0, 1] = _bias_of_distance(d + blk, tab_ref, h)


def _prompt_bias_tiles(rel_bias, blk):
    return pl.pallas_call(
        functools.partial(_prompt_bias_body, blk=blk),
        grid=(H_B,),
        in_specs=[pl.BlockSpec(memory_space=pltpu.SMEM)],
        out_specs=pl.BlockSpec((1, 2, blk, blk), lambda h: (h, 0, 0, 0)),
        out_shape=jax.ShapeDtypeStruct((H_B, 2, blk, blk), F32),
        compiler_params=_params(("arbitrary",)),
        name="prompt_bias",
    )(rel_bias.astype(F32).reshape(-1))


def _sample_bias_body(tab_ref, o_ref, *, t_new):
    h = pl.program_id(0)
    r = lax.broadcasted_iota(jnp.int32, (2 * t_new, PAGE_SIZE), 0)
    c = lax.broadcasted_iota(jnp.int32, (2 * t_new, PAGE_SIZE), 1)
    tok = jnp.where(r >= t_new, r - t_new, r)
    o_ref[0, 0] = _bias_of_distance(PAGE_SIZE + tok - c, tab_ref, h)
    d = tok - c
    o_ref[0, 1] = jnp.where(d >= 0, _bias_of_distance(jnp.maximum(d, 0), tab_ref, h), NEG_INF)


def _sample_bias_tiles(rel_bias, t_new):
    return pl.pallas_call(
        functools.partial(_sample_bias_body, t_new=t_new),
        grid=(H_B,),
        in_specs=[pl.BlockSpec(memory_space=pltpu.SMEM)],
        out_specs=pl.BlockSpec((1, 2, 2 * t_new, PAGE_SIZE), lambda h: (h, 0, 0, 0)),
        out_shape=jax.ShapeDtypeStruct((H_B, 2, 2 * t_new, PAGE_SIZE), F32),
        compiler_params=_params(("arbitrary",)),
        name="sample_bias",
    )(rel_bias.astype(F32).reshape(-1))


def _lambda(lam_ref, lam_init):
    lq = lam_ref[...]
    a = jnp.sum(lq[0:1] * lq[1:2], axis=1, keepdims=True)
    b = jnp.sum(lq[2:3] * lq[3:4], axis=1, keepdims=True)
    return jnp.exp(a) - jnp.exp(b) + lam_init


def _stack_maps(q):
    lane = lax.broadcasted_iota(jnp.int32, q.shape, 1)
    q = q * (DK_B ** -0.5)
    return jnp.concatenate([jnp.where(lane < DK_B, q, 0.0), jnp.where(lane >= DK_B, q, 0.0)], axis=0).astype(BF16)


def _diff_epilogue(w, t, lam, lam_init, sw):
    o = w[:t] - lam * w[t:]
    return o * lax.rsqrt(jnp.mean(o * o, -1, keepdims=True) + 1e-5) * sw * (1.0 - lam_init)


def _softmax_step(s, vs, m_scr, l_scr, acc_scr):
    m_old = m_scr[...]
    m_new = jnp.maximum(m_old, jnp.max(s, -1, keepdims=True))
    alpha = jnp.exp(m_old - m_new)
    p = jnp.exp(s - m_new)
    l_scr[...] = alpha * l_scr[...] + jnp.sum(p, -1, keepdims=True)
    p16 = p.astype(BF16)
    pv = None
    off = 0
    for v in vs:
        part = jnp.dot(p16[:, off:off + v.shape[0]], v, preferred_element_type=F32)
        pv = part if pv is None else pv + part
        off += v.shape[0]
    acc_scr[...] = alpha * acc_scr[...] + pv
    m_scr[...] = m_new


def _attn_prompt_body(far_ref, q_ref, k_ref, v_ref, bias_ref, lam_ref, sw_ref, o_ref, m_scr, l_scr, acc_scr,
                      *, blk, lam_init):
    h = pl.program_id(1)
    i = pl.program_id(2)
    qq = _stack_maps(q_ref[0])
    m_scr[...] = jnp.full(m_scr.shape, NEG_INF, F32)
    l_scr[...] = jnp.zeros(l_scr.shape, F32)
    acc_scr[...] = jnp.zeros(acc_scr.shape, F32)

    def tile(j, bias):
        kj = k_ref[0, pl.ds(pl.multiple_of(j * blk, blk), blk), :].astype(BF16)
        vj = v_ref[0, pl.ds(pl.multiple_of(j * blk, blk), blk), :].astype(BF16)
        _softmax_step(_dot_nt(qq, kj) + bias, [vj], m_scr, l_scr, acc_scr)

    far = far_ref[h]

    def far_body(j, carry):
        tile(j, far)
        return carry

    lax.fori_loop(0, jnp.maximum(i - 1, 0), far_body, 0)

    @pl.when(i >= 1)
    def _():
        b1 = bias_ref[0, 1]
        tile(i - 1, jnp.concatenate([b1, b1], axis=0))

    b0 = bias_ref[0, 0]
    tile(i, jnp.concatenate([b0, b0], axis=0))
    w = acc_scr[...] / l_scr[...]
    o_ref[0] = _diff_epilogue(w, blk, _lambda(lam_ref, lam_init), lam_init, sw_ref[...]).astype(o_ref.dtype)


def _attn_prompt(q, kv, rel_bias, lam_q, subln_w, lam_init, *, blk=256):
    B, T, _ = q.shape
    blk = min(blk, T)
    bias = _prompt_bias_tiles(rel_bias, blk)
    far = rel_bias[NUM_BUCKETS - 1].astype(F32)
    return pl.pallas_call(
        functools.partial(_attn_prompt_body, blk=blk, lam_init=lam_init),
        grid=(B, H_B, T // blk),
        in_specs=[pl.BlockSpec(memory_space=pltpu.SMEM),
                  pl.BlockSpec((1, blk, 2 * DK_B), lambda b, h, i: (b, i, h)),
                  pl.BlockSpec((1, T, 2 * DK_B), lambda b, h, i: (b, 0, h)),
                  pl.BlockSpec((1, T, DV_B), lambda b, h, i: (b, 0, H_B + h)),
                  pl.BlockSpec((1, 2, blk, blk), lambda b, h, i: (h, 0, 0, 0)),
                  pl.BlockSpec((4, DK_B), lambda b, h, i: (0, 0)),
                  pl.BlockSpec((1, DV_B), lambda b, h, i: (0, 0))],
        out_specs=pl.BlockSpec((1, blk, DV_B), lambda b, h, i: (b, i, h)),
        scratch_shapes=[pltpu.VMEM((2 * blk, 1), F32), pltpu.VMEM((2 * blk, 1), F32),
                        pltpu.VMEM((2 * blk, DV_B), F32)],
        out_shape=jax.ShapeDtypeStruct((B, T, H_B * DV_B), BF16),
        compiler_params=_params(("parallel", "parallel", "arbitrary")),
        name="attn_prompt",
    )(far, q, kv, kv, bias, lam_q.astype(F32), subln_w.reshape(1, DV_B).astype(F32))


def _attn_sample_body(*refs, pp, t_new, lam_init):
    pt_ref, far_ref, q_ref = refs[:3]
    k_refs = refs[3:3 + pp]
    v_refs = refs[3 + pp:3 + 2 * pp]
    kn_ref, vn_ref, bias_ref, lam_ref, sw_ref, o_ref, m_scr, l_scr, acc_scr = refs[3 + 2 * pp:]
    step = pl.program_id(1)
    last = step == pl.num_programs(1) - 1

    @pl.when(step == 0)
    def _():
        m_scr[...] = jnp.full(m_scr.shape, NEG_INF, F32)
        l_scr[...] = jnp.zeros(l_scr.shape, F32)
        acc_scr[...] = jnp.zeros(acc_scr.shape, F32)

    lam = _lambda(lam_ref, lam_init)
    for h in range(H_B):
        sl = slice(h * 2 * DK_B, (h + 1) * 2 * DK_B)
        qq = _stack_maps(q_ref[0, :, sl])
        far = far_ref[h]
        ss, vs = [], []
        for r in range(pp):
            kp = k_refs[r][0, :, h, :].astype(BF16)
            s = _dot_nt(qq, kp)
            if r == pp - 1:
                s = s + jnp.where(last, bias_ref[h, 0], far)
            else:
                s = s + far
            ss.append(s)
            vs.append(v_refs[r][0, :, h, :].astype(BF16))
        _softmax_step(jnp.concatenate(ss, axis=1), vs, m_scr.at[h], l_scr.at[h], acc_scr.at[h])

        @pl.when(last)
        def _():
            s = _dot_nt(qq, kn_ref[0, :, sl].astype(BF16)) + bias_ref[h, 1]
            _softmax_step(s, [vn_ref[0, :, sl].astype(BF16)], m_scr.at[h], l_scr.at[h], acc_scr.at[h])
            w = acc_scr[h] / l_scr[h]
            o_ref[0, :, sl] = _diff_epilogue(w, t_new, lam, lam_init, sw_ref[...]).astype(o_ref.dtype)


def _attn_sample(q, kv_new, cache_k, cache_v, page_table, rel_bias, lam_q, subln_w, lam_init, *, pp=4):
    Bd, t_new, _ = q.shape
    n_pages = page_table.shape[1]
    assert n_pages % pp == 0
    bias = _sample_bias_tiles(rel_bias, t_new)
    far = rel_bias[NUM_BUCKETS - 1].astype(F32)
    kv_pad = jnp.pad(kv_new, ((0, 0), (0, PAGE_SIZE - t_new), (0, 0)))
    page_spec = lambda r: pl.BlockSpec((1, PAGE_SIZE, H_B, 2 * DK_B),
                                       lambda b, s, pt, r=r: (pt[b, s * pp + r], 0, 0, 0))
    const = lambda *shape: pl.BlockSpec(shape, lambda b, s, pt: (0,) * len(shape))
    grid_spec = pltpu.PrefetchScalarGridSpec(
        num_scalar_prefetch=1,
        grid=(Bd, n_pages // pp),
        in_specs=([pl.BlockSpec(memory_space=pltpu.SMEM),
                   pl.BlockSpec((1, t_new, H_B * 2 * DK_B), lambda b, s, pt: (b, 0, 0))]
                  + [page_spec(r) for r in range(pp)] + [page_spec(r) for r in range(pp)]
                  + [pl.BlockSpec((1, PAGE_SIZE, H_B * 2 * DK_B), lambda b, s, pt: (b, 0, 0)),
                     pl.BlockSpec((1, PAGE_SIZE, H_B * DV_B), lambda b, s, pt: (b, 0, 1)),
                     const(H_B, 2, 2 * t_new, PAGE_SIZE), const(4, DK_B), const(1, DV_B)]),
        out_specs=pl.BlockSpec((1, t_new, H_B * DV_B), lambda b, s, pt: (b, 0, 0)),
        scratch_shapes=[pltpu.VMEM((H_B, 2 * t_new, 1), F32), pltpu.VMEM((H_B, 2 * t_new, 1), F32),
                        pltpu.VMEM((H_B, 2 * t_new, DV_B), F32)],
    )
    return pl.pallas_call(
        functools.partial(_attn_sample_body, pp=pp, t_new=t_new, lam_init=lam_init),
        grid_spec=grid_spec,
        out_shape=jax.ShapeDtypeStruct((Bd, t_new, H_B * DV_B), BF16),
        compiler_params=_params(("parallel", "arbitrary")),
        name="attn_sample",
    )(page_table, far, q, *([cache_k] * pp), *([cache_v] * pp), kv_pad, kv_pad, bias,
      lam_q.astype(F32), subln_w.reshape(1, DV_B).astype(F32))


def _gdn_mixer(x2d, B, T, w_in, conv_w, a_log, dt_bias, gnorm_w, conv_state, ssm_state):
    proj2d = _mm(x2d, w_in, tn=IN_A_PAD // 7)
    proj = proj2d.reshape(B, T, IN_A_PAD)
    state8 = jnp.pad(conv_state.astype(F32), ((0, 0), (8 - (CONV_W - 1), 0), (0, 0)))
    qkvc = _gdn_prep(proj, state8, conv_w.astype(F32), tt=512)
    bg = _gdn_gates(proj2d, a_log, dt_bias).reshape(B, T, LANES)
    xpad_tail = jnp.concatenate([conv_state.astype(F32), proj[:, :, :CONV_DIM]], axis=1)[:, -(CONV_W - 1):]
    if T % CHUNK:
        tp = -(-T // CHUNK) * CHUNK
        padt = lambda a: jnp.pad(a, ((0, 0), (0, tp - T), (0, 0)))
        y, s_new = _gdn_chunks(padt(qkvc), padt(proj[:, :, CONV_DIM:CONV_DIM + VDIM_A]), 0, padt(bg),
                               ssm_state, gnorm_w)
        y = y[:, :T]
    else:
        y, s_new = _gdn_chunks(qkvc, proj, CONV_DIM // (2 * DV_A), bg, ssm_state, gnorm_w)
    return y.reshape(B * T, VDIM_A), xpad_tail, s_new


def _mlp_and_embed(x, p2d, ln_g, ln_b, w_up, w_down, w_ple, w_pg):
    u = _mm(x, w_up, act="relu2", out_dtype=BF16, tn=1024)
    x = _mm_ln(u, w_down, x, ln_g, ln_b)
    return _ple_gate(x, p2d, w_ple, w_pg)


def kernel(x_prompt, x_sample, state_conv, state_ssm, cache_k, cache_v, page_table, p_prompt, p_sample,
           w_in_a, conv_w_a, a_log_a, dt_bias_a, gnorm_w_a, w_out_a, w_kv, w_q_b, lam_b, subln_w_b, w_o_b,
           rel_bias, ln_g, ln_b, w_up, w_down, w_ple, w_pg):
    bf = lambda w: w.astype(BF16)
    w_in = bf(jnp.pad(w_in_a, ((0, 0), (0, 0), (0, IN_A_PAD - IN_A))))
    w_out, w_kv16, w_q, w_o = bf(w_out_a), bf(w_kv), bf(w_q_b), bf(w_o_b)
    w_up16, w_down16, w_ple16, w_pg16 = bf(w_up), bf(w_down), bf(w_ple), bf(w_pg)

    def run(x3d, p, conv_state, ssm_state, attn_fn):
        B, T, _ = x3d.shape
        x = x3d.reshape(B * T, D_MODEL)
        convs, ssms = [], []
        kv = None
        for i in range(DEPTH):
            if i < N_A:
                y, cs, ss = _gdn_mixer(x, B, T, w_in[i], conv_w_a[i], a_log_a[i], dt_bias_a[i], gnorm_w_a[i],
                                       conv_state[i], ssm_state[i])
                convs.append(cs)
                ssms.append(ss)
                x = _mm_ln(y, w_out[i], x, ln_g[i, 0], ln_b[i, 0])
            else:
                j = i - N_A
                lam_init = 0.8 - 0.6 * math.exp(-0.3 * i)
                q = _mm(x, w_q[j]).reshape(B, T, H_B * 2 * DK_B)
                o = attn_fn(q, kv, lam_b[j], subln_w_b[j], lam_init)
                x = _mm_ln(o.reshape(B * T, H_B * DV_B), w_o[j], x, ln_g[i, 0], ln_b[i, 0])
            x = _mlp_and_embed(x, p[i].reshape(B * T, -1), ln_g[i, 1], ln_b[i, 1], w_up16[i], w_down16[i],
                               w_ple16[i], w_pg16[i])
            if i == N_A - 1:
                kv = _mm(x, w_kv16).reshape(B, T, -1)
        k_sh = kv[..., :H_B * 2 * DK_B].reshape(B, T, H_B, 2 * DK_B)
        v_sh = kv[..., H_B * 2 * DK_B:].reshape(B, T, H_B, DV_B)
        return x.reshape(B, T, D_MODEL), jnp.stack(convs), jnp.stack(ssms), k_sh, v_sh

    Bp = x_prompt.shape[0]
    conv0 = jnp.zeros((N_A, Bp, CONV_W - 1, CONV_DIM), x_prompt.dtype)
    ssm0 = jnp.zeros((N_A, Bp, H_V, DK_A, DV_A), state_ssm.dtype)
    y_prompt, conv_p, ssm_p, k_p, v_p = run(
        x_prompt, p_prompt, conv0, ssm0,
        lambda q, kv, lam_q, sw, li: _attn_prompt(q, kv, rel_bias, lam_q, sw, li))
    y_sample, conv_s, ssm_s, k_s, v_s = run(
        x_sample, p_sample, state_conv, state_ssm,
        lambda q, kv, lam_q, sw, li: _attn_sample(q, kv, cache_k, cache_v, page_table, rel_bias, lam_q, sw, li))
    return (y_prompt, y_sample, conv_p, ssm_p, k_p, v_p, conv_s, ssm_s, k_s, v_s)
```

```python
import functools
import math

import jax
import jax.numpy as jnp
from jax import lax
from jax.experimental import pallas as pl
from jax.experimental.pallas import tpu as pltpu

F32 = jnp.float32
BF16 = jnp.bfloat16

D_MODEL = 1024
DEPTH = 2
N_A = DEPTH // 2
PAGE_SIZE = 128
H_QK = 8
H_V = 16
DK_A = 128
DV_A = 128
CONV_W = 4
CHUNK = 64
QK_DIM = H_QK * DK_A
VDIM_A = H_V * DV_A
CONV_DIM = 2 * QK_DIM + VDIM_A
IN_A = CONV_DIM + VDIM_A + 2 * H_V
IN_A_PAD = CONV_DIM + VDIM_A + 128
H_B = 8
DK_B = 64
DV_B = 128
NUM_BUCKETS = 32
MAX_DISTANCE = 128
D_FF = 4 * D_MODEL
ALPHA = (2 * DEPTH) ** 0.25

LANES = 128
VMEM_LIMIT = 56 * 1024 * 1024
NEG_INF = float("-inf")


def _params(sem, vmem=VMEM_LIMIT):
    return pltpu.CompilerParams(dimension_semantics=sem, vmem_limit_bytes=vmem)


def _sigmoid(x):
    return 1.0 / (1.0 + jnp.exp(-x))


def _dot_nt(a, b):
    return lax.dot_general(a, b, (((1,), (1,)), ((), ())), preferred_element_type=F32)


def _dot_tn(a, b):
    return lax.dot_general(a, b, (((0,), (0,)), ((), ())), preferred_element_type=F32)


def _mm_body(x_ref, w_ref, *o_refs, act):
    acc = jnp.dot(x_ref[...].astype(BF16), w_ref[...], preferred_element_type=F32)
    if act == "relu2":
        r = jnp.maximum(acc, 0.0)
        acc = r * r
    for o_ref in o_refs:
        o_ref[...] = acc.astype(o_ref.dtype)


def _mm(x, w, *, act=None, out_dtypes=(F32,), tm=1024, tn=None):
    M, K = x.shape
    N = w.shape[1]
    tm = min(tm, M)
    tn = N if tn is None else tn
    assert M % tm == 0 and N % tn == 0
    outs = pl.pallas_call(
        functools.partial(_mm_body, act=act),
        grid=(M // tm, N // tn),
        in_specs=[pl.BlockSpec((tm, K), lambda i, j: (i, 0)),
                  pl.BlockSpec((K, tn), lambda i, j: (0, j))],
        out_specs=[pl.BlockSpec((tm, tn), lambda i, j: (i, j)) for _ in out_dtypes],
        out_shape=[jax.ShapeDtypeStruct((M, N), dt) for dt in out_dtypes],
        compiler_params=_params(("parallel", "arbitrary")),
        name="mm",
    )(x, w)
    return outs[0] if len(out_dtypes) == 1 else outs


def _mm_ln_body(x_ref, w_ref, r_ref, g_ref, b_ref, o_ref):
    acc = jnp.dot(x_ref[...].astype(BF16), w_ref[...], preferred_element_type=F32)
    y = ALPHA * r_ref[...] + acc
    mu = jnp.mean(y, -1, keepdims=True)
    yc = y - mu
    var = jnp.mean(yc * yc, -1, keepdims=True)
    o_ref[...] = yc * lax.rsqrt(var + 1e-5) * g_ref[...] + b_ref[...]


def _mm_ln(x, w, res, g, b, *, tm=512):
    M, K = x.shape
    N = w.shape[1]
    tm = min(tm, M)
    return pl.pallas_call(
        _mm_ln_body,
        grid=(M // tm,),
        in_specs=[pl.BlockSpec((tm, K), lambda i: (i, 0)),
                  pl.BlockSpec((K, N), lambda i: (0, 0)),
                  pl.BlockSpec((tm, N), lambda i: (i, 0)),
                  pl.BlockSpec((1, N), lambda i: (0, 0)),
                  pl.BlockSpec((1, N), lambda i: (0, 0))],
        out_specs=pl.BlockSpec((tm, N), lambda i: (i, 0)),
        out_shape=jax.ShapeDtypeStruct((M, N), F32),
        compiler_params=_params(("parallel",)),
        name="mm_ln",
    )(x, w, res, g.reshape(1, N), b.reshape(1, N))


def _ple_gate_body(x_ref, p_ref, wple_ref, wpg_ref, o_ref):
    x = x_ref[...]
    e = jnp.dot(p_ref[...].astype(BF16), wple_ref[...], preferred_element_type=F32)
    gt = jnp.dot(x.astype(BF16), wpg_ref[...], preferred_element_type=F32)
    o_ref[...] = x + e * _sigmoid(gt)


def _ple_gate(x, p, w_ple, w_pg, *, tm=512):
    M, D = x.shape
    P = p.shape[1]
    tm = min(tm, M)
    return pl.pallas_call(
        _ple_gate_body,
        grid=(M // tm,),
        in_specs=[pl.BlockSpec((tm, D), lambda i: (i, 0)),
                  pl.BlockSpec((tm, P), lambda i: (i, 0)),
                  pl.BlockSpec((P, D), lambda i: (0, 0)),
                  pl.BlockSpec((D, D), lambda i: (0, 0))],
        out_specs=pl.BlockSpec((tm, D), lambda i: (i, 0)),
        out_shape=jax.ShapeDtypeStruct((M, D), F32),
        compiler_params=_params(("parallel",)),
        name="ple_gate",
    )(x, p, w_ple, w_pg)


def _gdn_prep_body(x_ref, halo_ref, st_ref, cw_ref, o_ref, *, gw):
    t = pl.program_id(1)
    j = pl.program_id(2)
    prev = jnp.where(t == 0, st_ref[0], halo_ref[0])
    xf = jnp.concatenate([prev, x_ref[0]], axis=0)
    for h in range(gw // LANES):
        sl = slice(h * LANES, (h + 1) * LANES)
        xs = xf[:, sl]
        cw = cw_ref[:, sl]
        c = xs * cw[3:4]
        for s in (1, 2, 3):
            c = c + pltpu.roll(xs, s, 0) * cw[3 - s:4 - s]
        c = c[8:]
        c = c * _sigmoid(c)
        group = j * (gw // LANES) + h
        nrm = lax.rsqrt(jnp.sum(c * c, -1, keepdims=True) + 1e-6)
        scale = jnp.where(group < H_QK, nrm * (DK_A ** -0.5), jnp.where(group < 2 * H_QK, nrm, 1.0))
        o_ref[0, :, sl] = c * scale


def _gdn_prep(proj, state8, conv_w, *, tt, gw=512):
    B, T, _ = proj.shape
    tt = min(tt, T)
    hb = tt // 8
    return pl.pallas_call(
        functools.partial(_gdn_prep_body, gw=gw),
        grid=(B, T // tt, CONV_DIM // gw),
        in_specs=[pl.BlockSpec((1, tt, gw), lambda b, t, j: (b, t, j)),
                  pl.BlockSpec((1, 8, gw), lambda b, t, j: (b, jnp.maximum(t * hb - 1, 0), j)),
                  pl.BlockSpec((1, 8, gw), lambda b, t, j: (b, 0, j)),
                  pl.BlockSpec((CONV_W, gw), lambda b, t, j: (0, j))],
        out_specs=pl.BlockSpec((1, tt, gw), lambda b, t, j: (b, t, j)),
        out_shape=jax.ShapeDtypeStruct((B, T, CONV_DIM), F32),
        compiler_params=_params(("parallel", "parallel", "parallel")),
        name="gdn_prep",
    )(proj, proj, state8, conv_w)


def _gdn_gates_body(x_ref, alog_ref, dtb_ref, o_ref, *, period):
    ba = x_ref[...]
    lane = lax.broadcasted_iota(jnp.int32, ba.shape, 1)
    beta = _sigmoid(ba)
    xx = ba + dtb_ref[...]
    softplus = jnp.maximum(xx, 0.0) + jnp.log(1.0 + jnp.exp(-jnp.abs(xx)))
    g = -jnp.exp(alog_ref[...]) * softplus
    pos = lax.broadcasted_iota(jnp.int32, ba.shape, 0) % period
    gc = g
    s = 1
    while s < period:
        gc = gc + jnp.where(pos >= s, pltpu.roll(gc, s, 0), 0.0)
        s *= 2
    gc = pltpu.roll(gc, H_V, 1)
    o_ref[...] = jnp.where(lane < H_V, beta,
                           jnp.where(lane < 2 * H_V, g, jnp.where(lane < 3 * H_V, gc, 0.0)))


def _gdn_gates(proj2d, a_log, dt_bias, period, *, tm=1024):
    M = proj2d.shape[0]
    tm = min(tm, M)
    assert tm % period == 0
    pad = lambda v: jnp.pad(v.astype(F32), (H_V, LANES - 2 * H_V)).reshape(1, LANES)
    col = (CONV_DIM + VDIM_A) // LANES
    return pl.pallas_call(
        functools.partial(_gdn_gates_body, period=period),
        grid=(M // tm,),
        in_specs=[pl.BlockSpec((tm, LANES), lambda i: (i, col)),
                  pl.BlockSpec((1, LANES), lambda i: (0, 0)),
                  pl.BlockSpec((1, LANES), lambda i: (0, 0))],
        out_specs=pl.BlockSpec((tm, LANES), lambda i: (i, 0)),
        out_shape=jax.ShapeDtypeStruct((M, LANES), F32),
        compiler_params=_params(("parallel",)),
        name="gdn_gates",
    )(proj2d, pad(a_log), pad(dt_bias))


def _gdn_intra_body(q_ref, k_ref, v_ref, bgc_ref, gct_ref, u_ref, w_ref, qg_ref, kd_ref, aqk_ref, dec_ref, *, ca):
    hk = pl.program_id(2)
    C = CHUNK
    row = lax.broadcasted_iota(jnp.int32, (C, C), 0)
    col = lax.broadcasted_iota(jnp.int32, (C, C), 1)
    incl = row >= col
    strict = row > col
    eye = (row == col).astype(F32)
    lane = lax.broadcasted_iota(jnp.int32, (C, LANES), 1)
    chains = [(ci, e) for ci in range(ca) for e in range(2)]
    rows = [slice(ci * C, (ci + 1) * C) for ci in range(ca)]
    k16 = [k_ref[0, rs, :].astype(BF16) for rs in rows]
    kk = [_dot_nt(a, a) for a in k16]
    qk = [_dot_nt(q_ref[0, rs, :].astype(BF16), a) for rs, a in zip(rows, k16)]
    beta, gc, npow = [], [], []
    for ci, e in chains:
        hv = 2 * hk + e
        bgc = bgc_ref[0, rows[ci], :]
        b = jnp.sum(jnp.where(lane == hv, bgc, 0.0), axis=1, keepdims=True)
        g = jnp.sum(jnp.where(lane == 2 * H_V + hv, bgc, 0.0), axis=1, keepdims=True)
        dm = jnp.exp(jnp.where(incl, g - gct_ref[0, ci, pl.ds(hv, 1), :], NEG_INF))
        aqk_ref[0, rows[ci], e * C:(e + 1) * C] = jnp.where(incl, qk[ci] * dm, 0.0).astype(BF16)
        npow.append(jnp.where(strict, -(b * kk[ci] * dm), 0.0))
        beta.append(b)
        gc.append(g)
    tinv = [eye + n for n in npow]
    for _ in range(5):
        n16 = [n.astype(BF16) for n in npow]
        npow = [jnp.dot(a, a, preferred_element_type=F32) for a in n16]
        tinv = [t + jnp.dot(t.astype(BF16), n.astype(BF16), preferred_element_type=F32)
                for t, n in zip(tinv, npow)]
    eg = [jnp.exp(g) for g in gc]
    rhs = []
    for x, (ci, e) in enumerate(chains):
        k = k_ref[0, rows[ci], :]
        v = v_ref[0, rows[ci], e * DV_A:(e + 1) * DV_A]
        rhs.append(jnp.concatenate([v * beta[x], k * (beta[x] * eg[x])], axis=1).astype(BF16))
    uw = [jnp.dot(t.astype(BF16), r, preferred_element_type=F32) for t, r in zip(tinv, rhs)]
    for x, (ci, e) in enumerate(chains):
        rs = rows[ci]
        hs = slice(e * DV_A, (e + 1) * DV_A)
        u_ref[0, rs, hs] = uw[x][:, :DV_A].astype(BF16)
        w_ref[0, rs, hs] = uw[x][:, DV_A:].astype(BF16)
        qg_ref[0, rs, hs] = (q_ref[0, rs, :] * eg[x]).astype(BF16)
        g_last = gc[x][C - 1:C, :]
        kd_ref[0, rs, hs] = (k_ref[0, rs, :] * jnp.exp(g_last - gc[x])).astype(BF16)
        dec_ref[0, 0, ci, e:e + 1, :] = jnp.broadcast_to(jnp.exp(g_last), (1, LANES))


def _gdn_intra(qkvc, bgc, *, ca=8):
    B, T, _ = qkvc.shape
    nc = T // CHUNK
    ca = min(ca, nc)
    ta = ca * CHUNK
    gct = bgc[..., 2 * H_V:3 * H_V].reshape(B, nc, CHUNK, H_V).transpose(0, 1, 3, 2)
    kb = QK_DIM // DK_A
    wide = lambda: pl.BlockSpec((1, ta, 2 * DV_A), lambda b, t, h: (b, t, h))
    return pl.pallas_call(
        functools.partial(_gdn_intra_body, ca=ca),
        grid=(B, T // ta, H_QK),
        in_specs=[pl.BlockSpec((1, ta, DK_A), lambda b, t, h: (b, t, h)),
                  pl.BlockSpec((1, ta, DK_A), lambda b, t, h: (b, t, kb + h)),
                  pl.BlockSpec((1, ta, 2 * DV_A), lambda b, t, h: (b, t, kb + h)),
                  pl.BlockSpec((1, ta, LANES), lambda b, t, h: (b, t, 0)),
                  pl.BlockSpec((1, ca, H_V, CHUNK), lambda b, t, h: (b, t, 0, 0))],
        out_specs=[wide(), wide(), wide(), wide(),
                   pl.BlockSpec((1, ta, 2 * CHUNK), lambda b, t, h: (b, t, h)),
                   pl.BlockSpec((1, 1, ca, 2, LANES), lambda b, t, h: (b, h, t, 0, 0))],
        out_shape=[jax.ShapeDtypeStruct((B, T, VDIM_A), BF16)] * 4
        + [jax.ShapeDtypeStruct((B, T, H_V * CHUNK), BF16),
           jax.ShapeDtypeStruct((B, H_QK, nc, 2, LANES), F32)],
        compiler_params=_params(("parallel", "parallel", "arbitrary")),
        name="gdn_intra",
    )(qkvc, qkvc, qkvc, bgc, gct)


def _gdn_state_body(u_ref, w_ref, qg_ref, kd_ref, aqk_ref, dec_ref, z_ref, s0_ref, gw_ref, y_ref, st_ref, s_scr,
                    *, cb, hg):
    t = pl.program_id(2)
    C = CHUNK

    @pl.when(t == 0)
    def _():
        s_scr[...] = s0_ref[0]

    def chunk(ci, carry):
        rows = pl.ds(pl.multiple_of(ci * C, C), C)
        cols = [slice(hh * DV_A, (hh + 1) * DV_A) for hh in range(hg)]
        s = [s_scr[hh] for hh in range(hg)]
        r = [jnp.dot(jnp.concatenate([w_ref[0, rows, cs], qg_ref[0, rows, cs]], axis=0), s[hh].astype(BF16),
                     preferred_element_type=F32) for hh, cs in enumerate(cols)]
        v_new = [(u_ref[0, rows, cs].astype(F32) - r[hh][:C]).astype(BF16) for hh, cs in enumerate(cols)]
        s_upd = [_dot_tn(kd_ref[0, rows, cs], v_new[hh]) for hh, cs in enumerate(cols)]
        o_in = [jnp.dot(aqk_ref[0, rows, hh * C:(hh + 1) * C], v_new[hh], preferred_element_type=F32)
                for hh in range(hg)]
        for hh, cs in enumerate(cols):
            dec = dec_ref[0, hh // 2, pl.ds(ci, 1), hh % 2, :]
            s_scr[hh] = s[hh] * dec + s_upd[hh]
            o = r[hh][C:] + o_in[hh]
            z = z_ref[0, rows, cs]
            y = o * lax.rsqrt(jnp.mean(o * o, -1, keepdims=True) + 1e-6) * gw_ref[...] * (z * _sigmoid(z))
            y_ref[0, rows, cs] = y.astype(y_ref.dtype)
        return carry

    lax.fori_loop(0, cb, chunk, 0)

    @pl.when(t == pl.num_programs(2) - 1)
    def _():
        st_ref[0] = s_scr[...]


def _gdn_state(u, w, qg, kd, aqk, dec, z_arr, z_off, s0, gnorm_w, *, cb=8, hg=8):
    B, T, _ = u.shape
    nc = T // CHUNK
    cb = min(cb, nc)
    tb = cb * CHUNK
    ng = H_V // hg
    assert z_off % (hg * DV_A) == 0
    z_col = z_off // (hg * DV_A)
    wide = lambda: pl.BlockSpec((1, tb, hg * DV_A), lambda b, g, t: (b, t, g))
    return pl.pallas_call(
        functools.partial(_gdn_state_body, cb=cb, hg=hg),
        grid=(B, ng, T // tb),
        in_specs=[wide(), wide(), wide(), wide(),
                  pl.BlockSpec((1, tb, hg * CHUNK), lambda b, g, t: (b, t, g)),
                  pl.BlockSpec((1, hg // 2, cb, 2, LANES), lambda b, g, t: (b, g, t, 0, 0)),
                  pl.BlockSpec((1, tb, hg * DV_A), lambda b, g, t: (b, t, z_col + g)),
                  pl.BlockSpec((1, hg, DK_A, DV_A), lambda b, g, t: (b, g, 0, 0)),
                  pl.BlockSpec((1, DV_A), lambda b, g, t: (0, 0))],
        out_specs=[wide(), pl.BlockSpec((1, hg, DK_A, DV_A), lambda b, g, t: (b, g, 0, 0))],
        out_shape=[jax.ShapeDtypeStruct((B, T, VDIM_A), BF16),
                   jax.ShapeDtypeStruct((B, H_V, DK_A, DV_A), F32)],
        scratch_shapes=[pltpu.VMEM((hg, DK_A, DV_A), F32)],
        compiler_params=_params(("parallel", "parallel", "arbitrary")),
        name="gdn_state",
    )(u, w, qg, kd, aqk, dec, z_arr, s0, gnorm_w.reshape(1, DV_A).astype(F32))


def _bucket_starts():
    max_exact = NUM_BUCKETS // 2
    starts = list(range(max_exact + 1))
    n = max_exact
    for b in range(max_exact + 1, NUM_BUCKETS):
        while max_exact + int(math.log(n / max_exact) / math.log(MAX_DISTANCE / max_exact)
                              * (NUM_BUCKETS - max_exact)) < b:
            n += 1
        starts.append(n)
    return starts


_BUCKET_START = _bucket_starts()


def _bias_of_distance(n, tab_ref, h):
    val = jnp.full(n.shape, tab_ref[(NUM_BUCKETS - 1) * H_B + h], F32)
    for b in range(NUM_BUCKETS - 2, -1, -1):
        val = jnp.where(n < _BUCKET_START[b + 1], tab_ref[b * H_B + h], val)
    return val


def _prompt_bias_body(tab_ref, o_ref, *, blk):
    h = pl.program_id(0)
    key = lax.broadcasted_iota(jnp.int32, (blk, blk), 0)
    qry = lax.broadcasted_iota(jnp.int32, (blk, blk), 1)
    d = qry - key
    o_ref[0, 0] = jnp.where(d >= 0, _bias_of_distance(jnp.maximum(d, 0), tab_ref, h), NEG_INF)
    o_ref[0, 1] = _bias_of_distance(d + blk, tab_ref, h)


def _prompt_bias_tiles(rel_bias, blk):
    return pl.pallas_call(
        functools.partial(_prompt_bias_body, blk=blk),
        grid=(H_B,),
        in_specs=[pl.BlockSpec(memory_space=pltpu.SMEM)],
        out_specs=pl.BlockSpec((1, 2, blk, blk), lambda h: (h, 0, 0, 0)),
        out_shape=jax.ShapeDtypeStruct((H_B, 2, blk, blk), F32),
        compiler_params=_params(("arbitrary",)),
        name="prompt_bias",
    )(rel_bias.astype(F32).reshape(-1))


def _sample_bias_body(tab_ref, o_ref, *, t_new):
    h = pl.program_id(0)
    shape = (2 * t_new, PAGE_SIZE * H_B)
    r = lax.broadcasted_iota(jnp.int32, shape, 0)
    j = lax.broadcasted_iota(jnp.int32, shape, 1)
    tok = jnp.where(r >= t_new, r - t_new, r)
    c = j // H_B
    same_head = (j % H_B) == h
    far = jnp.full(shape, tab_ref[(NUM_BUCKETS - 1) * H_B + h], F32)
    o_ref[0] = jnp.where(same_head, far, NEG_INF)
    o_ref[1] = jnp.where(same_head, _bias_of_distance(PAGE_SIZE + tok - c, tab_ref, h), NEG_INF)
    d = tok - c
    o_ref[2] = jnp.where(same_head & (d >= 0), _bias_of_distance(jnp.maximum(d, 0), tab_ref, h), NEG_INF)


def _sample_bias_tiles(rel_bias, t_new):
    rows = 2 * t_new
    return pl.pallas_call(
        functools.partial(_sample_bias_body, t_new=t_new),
        grid=(H_B,),
        in_specs=[pl.BlockSpec(memory_space=pltpu.SMEM)],
        out_specs=pl.BlockSpec((3, rows, PAGE_SIZE * H_B), lambda h: (0, h, 0)),
        out_shape=jax.ShapeDtypeStruct((3, H_B * rows, PAGE_SIZE * H_B), F32),
        compiler_params=_params(("arbitrary",)),
        name="sample_bias",
    )(rel_bias.astype(F32).reshape(-1))


def _lambda(lam_ref, lam_init):
    lq = lam_ref[...]
    a = jnp.sum(lq[0:1] * lq[1:2], axis=1, keepdims=True)
    b = jnp.sum(lq[2:3] * lq[3:4], axis=1, keepdims=True)
    return jnp.exp(a) - jnp.exp(b) + lam_init


def _stack_maps(q):
    lane = lax.broadcasted_iota(jnp.int32, q.shape, 1)
    q = q * (DK_B ** -0.5)
    return jnp.concatenate([jnp.where(lane < DK_B, q, 0.0), jnp.where(lane >= DK_B, q, 0.0)], axis=0)


def _attn_prompt_body(far_ref, q_ref, k_ref, vt_ref, bias_ref, lam_ref, sw_ref, o_ref,
                      qt_scr, m_scr, l_scr, acc_scr, *, blk, lam_init):
    h = pl.program_id(1)
    i = pl.program_id(2)
    qt_scr[...] = _stack_maps(q_ref[0]).T.astype(BF16)
    m_scr[...] = jnp.full(m_scr.shape, NEG_INF, F32)
    l_scr[...] = jnp.zeros(l_scr.shape, F32)
    acc_scr[...] = jnp.zeros(acc_scr.shape, F32)
    far = far_ref[h]
    gw = min(blk, 256)

    def tile(j, bias):
        start = pl.multiple_of(j * blk, blk)
        kj = k_ref[0, pl.ds(start, blk), :]
        vtj = vt_ref[0, 0, :, pl.ds(start, blk)]
        groups = [slice(g * gw, (g + 1) * gw) for g in range(2 * blk // gw)]
        s = [jnp.dot(kj, qt_scr[:, cs], preferred_element_type=F32) for cs in groups]
        if bias is not None:
            s = [sg + bias[:, cs.start % blk:cs.start % blk + gw] for sg, cs in zip(s, groups)]
        p, alpha, m_new = [], [], []
        for sg, cs in zip(s, groups):
            m_old = m_scr[:, cs]
            m_cur = jnp.max(sg, axis=0, keepdims=True)
            if bias is None:
                m_cur = m_cur + far
            mn = jnp.maximum(m_old, m_cur)
            alpha.append(jnp.exp(m_old - mn))
            p.append(jnp.exp(sg - (mn - far if bias is None else mn)))
            m_new.append(mn)
        pv = [jnp.dot(vtj, pg.astype(BF16), preferred_element_type=F32) for pg in p]
        for g, cs in enumerate(groups):
            l_scr[:, cs] = alpha[g] * l_scr[:, cs] + jnp.sum(p[g], axis=0, keepdims=True)
            acc_scr[:, cs] = alpha[g] * acc_scr[:, cs] + pv[g]
            m_scr[:, cs] = m_new[g]

    def far_body(j, carry):
        tile(j, None)
        return carry

    lax.fori_loop(0, jnp.maximum(i - 1, 0), far_body, 0)

    @pl.when(i >= 1)
    def _():
        tile(i - 1, bias_ref[0, 1])

    tile(i, bias_ref[0, 0])
    wt = acc_scr[...] / l_scr[...]
    ot = wt[:, :blk] - _lambda(lam_ref, lam_init) * wt[:, blk:]
    ot = ot * lax.rsqrt(jnp.mean(ot * ot, axis=0, keepdims=True) + 1e-5)
    o_ref[0] = (ot.T * sw_ref[...] * (1.0 - lam_init)).astype(o_ref.dtype)


def _attn_prompt(q, k16, vt16, rel_bias, lam_q, subln_w, lam_init, *, blk=512):
    B, T, _ = q.shape
    blk = min(blk, T)
    bias = _prompt_bias_tiles(rel_bias, blk)
    far = rel_bias[NUM_BUCKETS - 1].astype(F32)
    return pl.pallas_call(
        functools.partial(_attn_prompt_body, blk=blk, lam_init=lam_init),
        grid=(B, H_B, T // blk),
        in_specs=[pl.BlockSpec(memory_space=pltpu.SMEM),
                  pl.BlockSpec((1, blk, 2 * DK_B), lambda b, h, i: (b, i, h)),
                  pl.BlockSpec((1, T, 2 * DK_B), lambda b, h, i: (b, 0, h)),
                  pl.BlockSpec((1, 1, DV_B, T), lambda b, h, i: (b, h, 0, 0)),
                  pl.BlockSpec((1, 2, blk, blk), lambda b, h, i: (h, 0, 0, 0)),
                  pl.BlockSpec((4, DK_B), lambda b, h, i: (0, 0)),
                  pl.BlockSpec((1, DV_B), lambda b, h, i: (0, 0))],
        out_specs=pl.BlockSpec((1, blk, DV_B), lambda b, h, i: (b, i, h)),
        scratch_shapes=[pltpu.VMEM((2 * DK_B, 2 * blk), BF16), pltpu.VMEM((1, 2 * blk), F32),
                        pltpu.VMEM((1, 2 * blk), F32), pltpu.VMEM((DV_B, 2 * blk), F32)],
        out_shape=jax.ShapeDtypeStruct((B, T, H_B * DV_B), BF16),
        compiler_params=_params(("parallel", "parallel", "arbitrary")),
        name="attn_prompt",
    )(far, q, k16, vt16, bias, lam_q.astype(F32), subln_w.reshape(1, DV_B).astype(F32))


def _attn_sample_body(*refs, pp, t_new, lam_init):
    pt_ref, q_ref = refs[:2]
    k_refs = refs[2:2 + pp]
    v_refs = refs[2 + pp:2 + 2 * pp]
    kn_ref, vn_ref, bias_ref, lam_ref, sw_ref, o_ref, m_scr, l_scr, acc_scr = refs[2 + 2 * pp:]
    step = pl.program_id(1)
    last = step == pl.num_programs(1) - 1
    rows = 2 * t_new
    flat = PAGE_SIZE * H_B

    @pl.when(step == 0)
    def _():
        m_scr[...] = jnp.full(m_scr.shape, NEG_INF, F32)
        l_scr[...] = jnp.zeros(l_scr.shape, F32)
        acc_scr[...] = jnp.zeros(acc_scr.shape, F32)

    qall = jnp.concatenate([_stack_maps(q_ref[0, :, h * 2 * DK_B:(h + 1) * 2 * DK_B]) for h in range(H_B)],
                           axis=0).astype(BF16)

    def update(scores, values):
        s = jnp.concatenate(scores, axis=1) if len(scores) > 1 else scores[0]
        m_old = m_scr[...]
        m_new = jnp.maximum(m_old, jnp.max(s, axis=1, keepdims=True))
        alpha = jnp.exp(m_old - m_new)
        p = jnp.exp(s - m_new)
        l_scr[...] = alpha * l_scr[...] + jnp.sum(p, axis=1, keepdims=True)
        p16 = p.astype(BF16)
        pv = None
        for r, v in enumerate(values):
            part = jnp.dot(p16[:, r * flat:(r + 1) * flat], v, preferred_element_type=F32)
            pv = part if pv is None else pv + part
        acc_scr[...] = alpha * acc_scr[...] + pv
        m_scr[...] = m_new

    def flat_page(ref):
        return ref[0].reshape(flat, 2 * DK_B).astype(BF16)

    scores, values = [], []
    for r in range(pp):
        s = _dot_nt(qall, flat_page(k_refs[r]))
        if r == pp - 1:
            s = s + bias_ref[last.astype(jnp.int32)]
        else:
            s = s + bias_ref[0]
        scores.append(s)
        values.append(flat_page(v_refs[r]))
    update(scores, values)

    @pl.when(last)
    def _():
        update([_dot_nt(qall, flat_page(kn_ref)) + bias_ref[2]], [flat_page(vn_ref)])
        w = acc_scr[...] / l_scr[...]
        lam = _lambda(lam_ref, lam_init)
        for h in range(H_B):
            o = w[h * rows:h * rows + t_new] - lam * w[h * rows + t_new:(h + 1) * rows]
            o = o * lax.rsqrt(jnp.mean(o * o, -1, keepdims=True) + 1e-5) * sw_ref[...] * (1.0 - lam_init)
            o_ref[0, :, h * DV_B:(h + 1) * DV_B] = o.astype(o_ref.dtype)


def _attn_sample(q, k_new, v_new, cache_k, cache_v, page_table, rel_bias, lam_q, subln_w, lam_init, *, pp=4):
    Bd, t_new, _ = q.shape
    n_pages = page_table.shape[1]
    assert n_pages % pp == 0
    bias = _sample_bias_tiles(rel_bias, t_new)
    padt = lambda a: jnp.pad(a, ((0, 0), (0, PAGE_SIZE - t_new), (0, 0), (0, 0)))
    page_spec = lambda r: pl.BlockSpec((1, PAGE_SIZE, H_B, 2 * DK_B),
                                       lambda b, s, pt, r=r: (pt[b, s * pp + r], 0, 0, 0))
    own_spec = pl.BlockSpec((1, PAGE_SIZE, H_B, 2 * DK_B), lambda b, s, pt: (b, 0, 0, 0))
    const = lambda *shape: pl.BlockSpec(shape, lambda b, s, pt: (0,) * len(shape))
    rows = H_B * 2 * t_new
    grid_spec = pltpu.PrefetchScalarGridSpec(
        num_scalar_prefetch=1,
        grid=(Bd, n_pages // pp),
        in_specs=([pl.BlockSpec((1, t_new, H_B * 2 * DK_B), lambda b, s, pt: (b, 0, 0))]
                  + [page_spec(r) for r in range(pp)] + [page_spec(r) for r in range(pp)]
                  + [own_spec, own_spec, const(3, rows, PAGE_SIZE * H_B), const(4, DK_B), const(1, DV_B)]),
        out_specs=pl.BlockSpec((1, t_new, H_B * DV_B), lambda b, s, pt: (b, 0, 0)),
        scratch_shapes=[pltpu.VMEM((rows, 1), F32), pltpu.VMEM((rows, 1), F32), pltpu.VMEM((rows, DV_B), F32)],
    )
    return pl.pallas_call(
        functools.partial(_attn_sample_body, pp=pp, t_new=t_new, lam_init=lam_init),
        grid_spec=grid_spec,
        out_shape=jax.ShapeDtypeStruct((Bd, t_new, H_B * DV_B), BF16),
        compiler_params=_params(("parallel", "arbitrary")),
        name="attn_sample",
    )(page_table, q, *([cache_k] * pp), *([cache_v] * pp), padt(k_new), padt(v_new), bias,
      lam_q.astype(F32), subln_w.reshape(1, DV_B).astype(F32))


def _gdn_mixer(x2d, B, T, w_in, conv_w, a_log, dt_bias, gnorm_w, conv_state, ssm_state):
    proj2d = _mm(x2d, w_in, tn=IN_A_PAD // 7)
    proj = proj2d.reshape(B, T, IN_A_PAD)
    state8 = jnp.pad(conv_state.astype(F32), ((0, 0), (8 - (CONV_W - 1), 0), (0, 0)))
    qkvc = _gdn_prep(proj, state8, conv_w.astype(F32), tt=512)
    bgc = _gdn_gates(proj2d, a_log, dt_bias, min(CHUNK, T)).reshape(B, T, LANES)
    xpad_tail = jnp.concatenate([conv_state.astype(F32), proj[:, :, :CONV_DIM]], axis=1)[:, -(CONV_W - 1):]
    if T % CHUNK:
        tp = -(-T // CHUNK) * CHUNK
        padt = lambda a: jnp.pad(a, ((0, 0), (0, tp - T), (0, 0)))
        lane = jnp.arange(LANES)
        is_gc = (lane >= 2 * H_V) & (lane < 3 * H_V)
        tail = jnp.broadcast_to(jnp.where(is_gc, bgc[:, T - 1:T], 0.0), (B, tp - T, LANES))
        bgc = jnp.concatenate([bgc, tail], axis=1)
        qkvc = padt(qkvc)
        z_arr, z_off = padt(proj[:, :, CONV_DIM:CONV_DIM + VDIM_A]), 0
    else:
        z_arr, z_off = proj, CONV_DIM
    u, w, qg, kd, aqk, dec = _gdn_intra(qkvc, bgc)
    y, s_new = _gdn_state(u, w, qg, kd, aqk, dec, z_arr, z_off, ssm_state, gnorm_w)
    return y[:, :T].reshape(B * T, VDIM_A), xpad_tail, s_new


def _mlp_and_embed(x, p2d, ln_g, ln_b, w_up, w_down, w_ple, w_pg):
    u = _mm(x, w_up, act="relu2", out_dtypes=(BF16,), tn=1024)
    x = _mm_ln(u, w_down, x, ln_g, ln_b)
    return _ple_gate(x, p2d, w_ple, w_pg)


def kernel(x_prompt, x_sample, state_conv, state_ssm, cache_k, cache_v, page_table, p_prompt, p_sample,
           w_in_a, conv_w_a, a_log_a, dt_bias_a, gnorm_w_a, w_out_a, w_kv, w_q_b, lam_b, subln_w_b, w_o_b,
           rel_bias, ln_g, ln_b, w_up, w_down, w_ple, w_pg):
    bf = lambda w: w.astype(BF16)
    w_in = bf(jnp.pad(w_in_a, ((0, 0), (0, 0), (0, IN_A_PAD - IN_A))))
    w_out, w_kv16, w_q, w_o = bf(w_out_a), bf(w_kv), bf(w_q_b), bf(w_o_b)
    w_up16, w_down16, w_ple16, w_pg16 = bf(w_up), bf(w_down), bf(w_ple), bf(w_pg)

    def run(x3d, p, conv_state, ssm_state, attn_fn):
        B, T, _ = x3d.shape
        x = x3d.reshape(B * T, D_MODEL)
        convs, ssms = [], []
        kv = kv16 = None
        for i in range(DEPTH):
            if i < N_A:
                y, cs, ss = _gdn_mixer(x, B, T, w_in[i], conv_w_a[i], a_log_a[i], dt_bias_a[i], gnorm_w_a[i],
                                       conv_state[i], ssm_state[i])
                convs.append(cs)
                ssms.append(ss)
                x = _mm_ln(y, w_out[i], x, ln_g[i, 0], ln_b[i, 0])
            else:
                j = i - N_A
                lam_init = 0.8 - 0.6 * math.exp(-0.3 * i)
                q = _mm(x, w_q[j]).reshape(B, T, H_B * 2 * DK_B)
                o = attn_fn(q, kv, kv16, lam_b[j], subln_w_b[j], lam_init)
                x = _mm_ln(o.reshape(B * T, H_B * DV_B), w_o[j], x, ln_g[i, 0], ln_b[i, 0])
            x = _mlp_and_embed(x, p[i].reshape(B * T, -1), ln_g[i, 1], ln_b[i, 1], w_up16[i], w_down16[i],
                               w_ple16[i], w_pg16[i])
            if i == N_A - 1:
                kv, kv16 = _mm(x, w_kv16, out_dtypes=(F32, BF16), tn=1024)
                kv, kv16 = kv.reshape(B, T, -1), kv16.reshape(B, T, -1)
        k_sh = kv[..., :H_B * 2 * DK_B].reshape(B, T, H_B, 2 * DK_B)
        v_sh = kv[..., H_B * 2 * DK_B:].reshape(B, T, H_B, DV_B)
        return x.reshape(B, T, D_MODEL), jnp.stack(convs), jnp.stack(ssms), k_sh, v_sh

    def attn_prompt(q, kv, kv16, lam_q, sw, lam_init):
        B, T, _ = q.shape
        vt16 = kv16[..., H_B * 2 * DK_B:].reshape(B, T, H_B, DV_B).transpose(0, 2, 3, 1)
        return _attn_prompt(q, kv16, vt16, rel_bias, lam_q, sw, lam_init)

    def attn_sample(q, kv, kv16, lam_q, sw, lam_init):
        B, T, _ = q.shape
        k_new = kv[..., :H_B * 2 * DK_B].reshape(B, T, H_B, 2 * DK_B)
        v_new = kv[..., H_B * 2 * DK_B:].reshape(B, T, H_B, DV_B)
        return _attn_sample(q, k_new, v_new, cache_k, cache_v, page_table, rel_bias, lam_q, sw, lam_init)

    Bp = x_prompt.shape[0]
    conv0 = jnp.zeros((N_A, Bp, CONV_W - 1, CONV_DIM), x_prompt.dtype)
    ssm0 = jnp.zeros((N_A, Bp, H_V, DK_A, DV_A), state_ssm.dtype)
    y_prompt, conv_p, ssm_p, k_p, v_p = run(x_prompt, p_prompt, conv0, ssm0, attn_prompt)
    y_sample, conv_s, ssm_s, k_s, v_s = run(x_sample, p_sample, state_conv, state_ssm, attn_sample)
    return (y_prompt, y_sample, conv_p, ssm_p, k_p, v_p, conv_s, ssm_s, k_s, v_s)
```

```python
import functools
import math

import jax
import jax.numpy as jnp
from jax import lax
from jax.experimental import pallas as pl
from jax.experimental.pallas import tpu as pltpu

F32 = jnp.float32
BF16 = jnp.bfloat16

D_MODEL = 1024
DEPTH = 2
N_A = DEPTH // 2
PAGE_SIZE = 128
H_QK = 8
H_V = 16
DK_A = 128
DV_A = 128
CONV_W = 4
CHUNK = 64
QK_DIM = H_QK * DK_A
VDIM_A = H_V * DV_A
CONV_DIM = 2 * QK_DIM + VDIM_A
IN_A = CONV_DIM + VDIM_A + 2 * H_V
IN_A_PAD = CONV_DIM + VDIM_A + 128
H_B = 8
DK_B = 64
DV_B = 128
NUM_BUCKETS = 32
MAX_DISTANCE = 128
D_FF = 4 * D_MODEL
ALPHA = (2 * DEPTH) ** 0.25

LANES = 128
VMEM_LIMIT = 56 * 1024 * 1024
NEG_INF = float("-inf")
LOG2E = math.log2(math.e)


def _params(sem, vmem=VMEM_LIMIT):
    return pltpu.CompilerParams(dimension_semantics=sem, vmem_limit_bytes=vmem)


def _sigmoid(x):
    return 1.0 / (1.0 + jnp.exp(-x))


def _dot_nt(a, b):
    return lax.dot_general(a, b, (((1,), (1,)), ((), ())), preferred_element_type=F32)


def _dot_tn(a, b):
    return lax.dot_general(a, b, (((0,), (0,)), ((), ())), preferred_element_type=F32)


def _mm_body(x_ref, w_ref, *o_refs, act):
    acc = jnp.dot(x_ref[...].astype(BF16), w_ref[...], preferred_element_type=F32)
    if act == "relu2":
        r = jnp.maximum(acc, 0.0)
        acc = r * r
    for o_ref in o_refs:
        o_ref[...] = acc.astype(o_ref.dtype)


def _mm(x, w, *, act=None, out_dtypes=(F32,), tm=1024, tn=None):
    M, K = x.shape
    N = w.shape[1]
    tm = min(tm, M)
    tn = N if tn is None else tn
    assert M % tm == 0 and N % tn == 0
    outs = pl.pallas_call(
        functools.partial(_mm_body, act=act),
        grid=(M // tm, N // tn),
        in_specs=[pl.BlockSpec((tm, K), lambda i, j: (i, 0)),
                  pl.BlockSpec((K, tn), lambda i, j: (0, j))],
        out_specs=[pl.BlockSpec((tm, tn), lambda i, j: (i, j)) for _ in out_dtypes],
        out_shape=[jax.ShapeDtypeStruct((M, N), dt) for dt in out_dtypes],
        compiler_params=_params(("parallel", "arbitrary")),
        name="mm",
    )(x, w)
    return outs[0] if len(out_dtypes) == 1 else outs


def _mm_ln_body(x_ref, w_ref, r_ref, g_ref, b_ref, o_ref):
    acc = jnp.dot(x_ref[...].astype(BF16), w_ref[...], preferred_element_type=F32)
    y = ALPHA * r_ref[...] + acc
    mu = jnp.mean(y, -1, keepdims=True)
    yc = y - mu
    var = jnp.mean(yc * yc, -1, keepdims=True)
    o_ref[...] = yc * lax.rsqrt(var + 1e-5) * g_ref[...] + b_ref[...]


def _mm_ln(x, w, res, g, b, *, tm=512):
    M, K = x.shape
    N = w.shape[1]
    tm = min(tm, M)
    return pl.pallas_call(
        _mm_ln_body,
        grid=(M // tm,),
        in_specs=[pl.BlockSpec((tm, K), lambda i: (i, 0)),
                  pl.BlockSpec((K, N), lambda i: (0, 0)),
                  pl.BlockSpec((tm, N), lambda i: (i, 0)),
                  pl.BlockSpec((1, N), lambda i: (0, 0)),
                  pl.BlockSpec((1, N), lambda i: (0, 0))],
        out_specs=pl.BlockSpec((tm, N), lambda i: (i, 0)),
        out_shape=jax.ShapeDtypeStruct((M, N), F32),
        compiler_params=_params(("parallel",)),
        name="mm_ln",
    )(x, w, res, g.reshape(1, N), b.reshape(1, N))


def _ple_gate_body(x_ref, p_ref, wple_ref, wpg_ref, o_ref):
    x = x_ref[...]
    e = jnp.dot(p_ref[...].astype(BF16), wple_ref[...], preferred_element_type=F32)
    gt = jnp.dot(x.astype(BF16), wpg_ref[...], preferred_element_type=F32)
    o_ref[...] = x + e * _sigmoid(gt)


def _ple_gate(x, p, w_ple, w_pg, *, tm=512):
    M, D = x.shape
    P = p.shape[1]
    tm = min(tm, M)
    return pl.pallas_call(
        _ple_gate_body,
        grid=(M // tm,),
        in_specs=[pl.BlockSpec((tm, D), lambda i: (i, 0)),
                  pl.BlockSpec((tm, P), lambda i: (i, 0)),
                  pl.BlockSpec((P, D), lambda i: (0, 0)),
                  pl.BlockSpec((D, D), lambda i: (0, 0))],
        out_specs=pl.BlockSpec((tm, D), lambda i: (i, 0)),
        out_shape=jax.ShapeDtypeStruct((M, D), F32),
        compiler_params=_params(("parallel",)),
        name="ple_gate",
    )(x, p, w_ple, w_pg)


def _conv_silu_norm(prev, x, cw_ref, store, first_group):
    xf = jnp.concatenate([prev, x], axis=0)
    for h in range(x.shape[1] // LANES):
        sl = slice(h * LANES, (h + 1) * LANES)
        xs = xf[:, sl]
        cw = cw_ref[:, sl]
        c = xs * cw[3:4]
        for s in (1, 2, 3):
            c = c + pltpu.roll(xs, s, 0) * cw[3 - s:4 - s]
        c = c[8:]
        c = c * _sigmoid(c)
        group = first_group + h
        nrm = lax.rsqrt(jnp.sum(c * c, -1, keepdims=True) + 1e-6)
        scale = jnp.where(group < H_QK, nrm * (DK_A ** -0.5), jnp.where(group < 2 * H_QK, nrm, 1.0))
        store(sl, c * scale)


def _gdn_prep_body(x_ref, halo_ref, st_ref, cw_ref, o_ref, *, gw):
    t = pl.program_id(1)
    j = pl.program_id(2)
    prev = jnp.where(t == 0, st_ref[0], halo_ref[0])

    def store(sl, val):
        o_ref[0, :, sl] = val

    _conv_silu_norm(prev, x_ref[0], cw_ref, store, j * (gw // LANES))


def _inproj_conv_body(x_ref, w_ref, st_ref, cw_ref, o_ref, tail_ref, carry_scr, *, tiles_per_seq):
    i = pl.program_id(0)
    j = pl.program_id(1)
    tn = w_ref.shape[1]
    acc = jnp.dot(x_ref[...].astype(BF16), w_ref[...], preferred_element_type=F32)
    prev = jnp.where(i % tiles_per_seq == 0, st_ref[0], carry_scr[j])

    def store(sl, val):
        o_ref[:, sl] = val

    _conv_silu_norm(prev, acc, cw_ref, store, j * (tn // LANES))
    carry_scr[j] = acc[acc.shape[0] - 8:]
    tail_ref[0] = acc[acc.shape[0] - 8:]


def _inproj_conv(x2d, w_qkv, state8, conv_w, T, *, tm=512, tn=1024):
    M, K = x2d.shape
    assert T % tm == 0 and CONV_DIM % tn == 0
    tps = T // tm
    qkvc, tails = pl.pallas_call(
        functools.partial(_inproj_conv_body, tiles_per_seq=tps),
        grid=(M // tm, CONV_DIM // tn),
        in_specs=[pl.BlockSpec((tm, K), lambda i, j: (i, 0)),
                  pl.BlockSpec((K, tn), lambda i, j: (0, j)),
                  pl.BlockSpec((1, 8, tn), lambda i, j: (i // tps, 0, j)),
                  pl.BlockSpec((CONV_W, tn), lambda i, j: (0, j))],
        out_specs=[pl.BlockSpec((tm, tn), lambda i, j: (i, j)),
                   pl.BlockSpec((1, 8, tn), lambda i, j: (i, 0, j))],
        out_shape=[jax.ShapeDtypeStruct((M, CONV_DIM), F32), jax.ShapeDtypeStruct((M // tm, 8, CONV_DIM), F32)],
        scratch_shapes=[pltpu.VMEM((CONV_DIM // tn, 8, tn), F32)],
        compiler_params=_params(("arbitrary", "arbitrary")),
        name="inproj_conv",
    )(x2d, w_qkv, state8, conv_w)
    return qkvc, tails[tps - 1::tps]


def _gdn_prep(proj, state8, conv_w, *, tt, gw=512):
    B, T, _ = proj.shape
    tt = min(tt, T)
    hb = tt // 8
    return pl.pallas_call(
        functools.partial(_gdn_prep_body, gw=gw),
        grid=(B, T // tt, CONV_DIM // gw),
        in_specs=[pl.BlockSpec((1, tt, gw), lambda b, t, j: (b, t, j)),
                  pl.BlockSpec((1, 8, gw), lambda b, t, j: (b, jnp.maximum(t * hb - 1, 0), j)),
                  pl.BlockSpec((1, 8, gw), lambda b, t, j: (b, 0, j)),
                  pl.BlockSpec((CONV_W, gw), lambda b, t, j: (0, j))],
        out_specs=pl.BlockSpec((1, tt, gw), lambda b, t, j: (b, t, j)),
        out_shape=jax.ShapeDtypeStruct((B, T, CONV_DIM), F32),
        compiler_params=_params(("parallel", "parallel", "parallel")),
        name="gdn_prep",
    )(proj, proj, state8, conv_w)


def _gdn_gates_body(x_ref, alog_ref, dtb_ref, o_ref, *, period):
    ba = x_ref[...]
    lane = lax.broadcasted_iota(jnp.int32, ba.shape, 1)
    beta = _sigmoid(ba)
    xx = ba + dtb_ref[...]
    softplus = jnp.maximum(xx, 0.0) + jnp.log(1.0 + jnp.exp(-jnp.abs(xx)))
    g = -jnp.exp(alog_ref[...]) * softplus
    pos = lax.broadcasted_iota(jnp.int32, ba.shape, 0) % period
    gc = g
    s = 1
    while s < period:
        gc = gc + jnp.where(pos >= s, pltpu.roll(gc, s, 0), 0.0)
        s *= 2
    gc = pltpu.roll(gc, H_V, 1)
    o_ref[...] = jnp.where(lane < H_V, beta,
                           jnp.where(lane < 2 * H_V, g, jnp.where(lane < 3 * H_V, gc, 0.0)))


def _gdn_gates(proj2d, col, a_log, dt_bias, period, *, tm=1024):
    M = proj2d.shape[0]
    tm = min(tm, M)
    assert tm % period == 0
    pad = lambda v: jnp.pad(v.astype(F32), (H_V, LANES - 2 * H_V)).reshape(1, LANES)
    return pl.pallas_call(
        functools.partial(_gdn_gates_body, period=period),
        grid=(M // tm,),
        in_specs=[pl.BlockSpec((tm, LANES), lambda i: (i, col)),
                  pl.BlockSpec((1, LANES), lambda i: (0, 0)),
                  pl.BlockSpec((1, LANES), lambda i: (0, 0))],
        out_specs=pl.BlockSpec((tm, LANES), lambda i: (i, 0)),
        out_shape=jax.ShapeDtypeStruct((M, LANES), F32),
        compiler_params=_params(("parallel",)),
        name="gdn_gates",
    )(proj2d, pad(a_log), pad(dt_bias))


def _gdn_intra_body(q_ref, k_ref, v_ref, bgc_ref, gct_ref, u_ref, w_ref, qg_ref, kd_ref, aqk_ref, dec_ref,
                    *, nb, ca):
    hk = pl.program_id(2)
    C = CHUNK
    row = lax.broadcasted_iota(jnp.int32, (C, C), 0)
    col = lax.broadcasted_iota(jnp.int32, (C, C), 1)
    incl = row >= col
    strict = row > col
    eye = (row == col).astype(F32)
    lane = lax.broadcasted_iota(jnp.int32, (C, LANES), 1)
    blocks = [(bi, ci) for bi in range(nb) for ci in range(ca)]
    chains = [(x, e) for x in range(len(blocks)) for e in range(2)]
    at = lambda x: (blocks[x][0], slice(blocks[x][1] * C, (blocks[x][1] + 1) * C))
    full = (slice(None),)
    k16 = [k_ref[at(x) + full].astype(BF16) for x in range(len(blocks))]
    kk = [_dot_nt(a, a) for a in k16]
    qk = [_dot_nt(q_ref[at(x) + full].astype(BF16), k16[x]) for x in range(len(blocks))]
    beta, gc, npow = [], [], []
    for x, e in chains:
        bi, ci = blocks[x]
        hv = 2 * hk + e
        bgc = bgc_ref[at(x) + full]
        b = jnp.sum(jnp.where(lane == hv, bgc, 0.0), axis=1, keepdims=True)
        g = jnp.sum(jnp.where(lane == 2 * H_V + hv, bgc, 0.0), axis=1, keepdims=True)
        dm = jnp.exp(jnp.where(incl, g - gct_ref[bi, ci, pl.ds(hv, 1), :], NEG_INF))
        aqk_ref[at(x) + (slice(e * C, (e + 1) * C),)] = jnp.where(incl, qk[x] * dm, 0.0).astype(BF16)
        npow.append(jnp.where(strict, -(b * kk[x] * dm), 0.0))
        beta.append(b)
        gc.append(g)
    tinv = [eye + n for n in npow]
    for _ in range(5):
        n16 = [n.astype(BF16) for n in npow]
        npow = [jnp.dot(a, a, preferred_element_type=F32) for a in n16]
        tinv = [t + jnp.dot(t.astype(BF16), n.astype(BF16), preferred_element_type=F32)
                for t, n in zip(tinv, npow)]
    eg = [jnp.exp(g) for g in gc]
    rhs = []
    for y, (x, e) in enumerate(chains):
        k = k_ref[at(x) + full]
        v = v_ref[at(x) + (slice(e * DV_A, (e + 1) * DV_A),)]
        rhs.append(jnp.concatenate([v * beta[y], k * (beta[y] * eg[y])], axis=1).astype(BF16))
    uw = [jnp.dot(t.astype(BF16), r, preferred_element_type=F32) for t, r in zip(tinv, rhs)]
    for y, (x, e) in enumerate(chains):
        bi, ci = blocks[x]
        hs = at(x) + (slice(e * DV_A, (e + 1) * DV_A),)
        u_ref[hs] = uw[y][:, :DV_A].astype(BF16)
        w_ref[hs] = uw[y][:, DV_A:].astype(BF16)
        qg_ref[hs] = (q_ref[at(x) + full] * eg[y]).astype(BF16)
        g_last = gc[y][C - 1:C, :]
        kd_ref[hs] = (k_ref[at(x) + full] * jnp.exp(g_last - gc[y])).astype(BF16)
        dec_ref[bi, 0, ci, e:e + 1, :] = jnp.broadcast_to(jnp.exp(g_last), (1, LANES))


def _gdn_intra(qkvc, bgc, *, chains=16):
    B, T, _ = qkvc.shape
    nc = T // CHUNK
    ca = min(chains // 2, nc)
    nb = min(max(chains // (2 * ca), 1), B)
    assert nc % ca == 0 and B % nb == 0
    ta = ca * CHUNK
    gct = bgc[..., 2 * H_V:3 * H_V].reshape(B, nc, CHUNK, H_V).transpose(0, 1, 3, 2)
    kb = QK_DIM // DK_A
    wide = lambda: pl.BlockSpec((nb, ta, 2 * DV_A), lambda b, t, h: (b, t, h))
    return pl.pallas_call(
        functools.partial(_gdn_intra_body, nb=nb, ca=ca),
        grid=(B // nb, T // ta, H_QK),
        in_specs=[pl.BlockSpec((nb, ta, DK_A), lambda b, t, h: (b, t, h)),
                  pl.BlockSpec((nb, ta, DK_A), lambda b, t, h: (b, t, kb + h)),
                  pl.BlockSpec((nb, ta, 2 * DV_A), lambda b, t, h: (b, t, kb + h)),
                  pl.BlockSpec((nb, ta, LANES), lambda b, t, h: (b, t, 0)),
                  pl.BlockSpec((nb, ca, H_V, CHUNK), lambda b, t, h: (b, t, 0, 0))],
        out_specs=[wide(), wide(), wide(), wide(),
                   pl.BlockSpec((nb, ta, 2 * CHUNK), lambda b, t, h: (b, t, h)),
                   pl.BlockSpec((nb, 1, ca, 2, LANES), lambda b, t, h: (b, h, t, 0, 0))],
        out_shape=[jax.ShapeDtypeStruct((B, T, VDIM_A), BF16)] * 4
        + [jax.ShapeDtypeStruct((B, T, H_V * CHUNK), BF16),
           jax.ShapeDtypeStruct((B, H_QK, nc, 2, LANES), F32)],
        compiler_params=_params(("parallel", "parallel", "arbitrary")),
        name="gdn_intra",
    )(qkvc, qkvc, qkvc, bgc, gct)


def _gdn_state_body(u_ref, w_ref, qg_ref, kd_ref, aqk_ref, dec_ref, z_ref, s0_ref, gw_ref, y_ref, st_ref, s_scr,
                    *, cb, hg):
    t = pl.program_id(2)
    C = CHUNK

    @pl.when(t == 0)
    def _():
        s_scr[...] = s0_ref[0]

    def chunk(ci, carry):
        rows = pl.ds(pl.multiple_of(ci * C, C), C)
        cols = [slice(hh * DV_A, (hh + 1) * DV_A) for hh in range(hg)]
        s = [s_scr[hh] for hh in range(hg)]
        r = [jnp.dot(jnp.concatenate([w_ref[0, rows, cs], qg_ref[0, rows, cs]], axis=0), s[hh].astype(BF16),
                     preferred_element_type=F32) for hh, cs in enumerate(cols)]
        v_new = [(u_ref[0, rows, cs].astype(F32) - r[hh][:C]).astype(BF16) for hh, cs in enumerate(cols)]
        s_upd = [_dot_tn(kd_ref[0, rows, cs], v_new[hh]) for hh, cs in enumerate(cols)]
        o_in = [jnp.dot(aqk_ref[0, rows, hh * C:(hh + 1) * C], v_new[hh], preferred_element_type=F32)
                for hh in range(hg)]
        for hh, cs in enumerate(cols):
            dec = dec_ref[0, hh // 2, pl.ds(ci, 1), hh % 2, :]
            s_scr[hh] = s[hh] * dec + s_upd[hh]
            o = r[hh][C:] + o_in[hh]
            z = z_ref[0, rows, cs]
            y = o * lax.rsqrt(jnp.mean(o * o, -1, keepdims=True) + 1e-6) * gw_ref[...] * (z * _sigmoid(z))
            y_ref[0, rows, cs] = y.astype(y_ref.dtype)
        return carry

    lax.fori_loop(0, cb, chunk, 0)

    @pl.when(t == pl.num_programs(2) - 1)
    def _():
        st_ref[0] = s_scr[...]


def _gdn_state(u, w, qg, kd, aqk, dec, z_arr, z_off, s0, gnorm_w, *, cb=8, hg=8):
    B, T, _ = u.shape
    nc = T // CHUNK
    cb = min(cb, nc)
    tb = cb * CHUNK
    ng = H_V // hg
    assert z_off % (hg * DV_A) == 0
    z_col = z_off // (hg * DV_A)
    wide = lambda: pl.BlockSpec((1, tb, hg * DV_A), lambda b, g, t: (b, t, g))
    return pl.pallas_call(
        functools.partial(_gdn_state_body, cb=cb, hg=hg),
        grid=(B, ng, T // tb),
        in_specs=[wide(), wide(), wide(), wide(),
                  pl.BlockSpec((1, tb, hg * CHUNK), lambda b, g, t: (b, t, g)),
                  pl.BlockSpec((1, hg // 2, cb, 2, LANES), lambda b, g, t: (b, g, t, 0, 0)),
                  pl.BlockSpec((1, tb, hg * DV_A), lambda b, g, t: (b, t, z_col + g)),
                  pl.BlockSpec((1, hg, DK_A, DV_A), lambda b, g, t: (b, g, 0, 0)),
                  pl.BlockSpec((1, DV_A), lambda b, g, t: (0, 0))],
        out_specs=[wide(), pl.BlockSpec((1, hg, DK_A, DV_A), lambda b, g, t: (b, g, 0, 0))],
        out_shape=[jax.ShapeDtypeStruct((B, T, VDIM_A), BF16),
                   jax.ShapeDtypeStruct((B, H_V, DK_A, DV_A), F32)],
        scratch_shapes=[pltpu.VMEM((hg, DK_A, DV_A), F32)],
        compiler_params=_params(("parallel", "parallel", "arbitrary")),
        name="gdn_state",
    )(u, w, qg, kd, aqk, dec, z_arr, s0, gnorm_w.reshape(1, DV_A).astype(F32))


def _bucket_starts():
    max_exact = NUM_BUCKETS // 2
    starts = list(range(max_exact + 1))
    n = max_exact
    for b in range(max_exact + 1, NUM_BUCKETS):
        while max_exact + int(math.log(n / max_exact) / math.log(MAX_DISTANCE / max_exact)
                              * (NUM_BUCKETS - max_exact)) < b:
            n += 1
        starts.append(n)
    return starts


_BUCKET_START = _bucket_starts()


def _bias_of_distance(n, tab_ref, h):
    val = jnp.full(n.shape, tab_ref[(NUM_BUCKETS - 1) * H_B + h], F32)
    for b in range(NUM_BUCKETS - 2, -1, -1):
        val = jnp.where(n < _BUCKET_START[b + 1], tab_ref[b * H_B + h], val)
    return val


def _prompt_bias_body(tab_ref, o_ref, *, blk):
    h = pl.program_id(0)
    key = lax.broadcasted_iota(jnp.int32, (blk, blk), 0)
    qry = lax.broadcasted_iota(jnp.int32, (blk, blk), 1)
    d = qry - key
    o_ref[0, 0] = LOG2E * _bias_of_distance(d + blk, tab_ref, h)
    o_ref[0, 1] = jnp.where(d >= 0, LOG2E * _bias_of_distance(jnp.maximum(d, 0), tab_ref, h), NEG_INF)


def _prompt_bias_tiles(rel_bias, blk):
    return pl.pallas_call(
        functools.partial(_prompt_bias_body, blk=blk),
        grid=(H_B,),
        in_specs=[pl.BlockSpec(memory_space=pltpu.SMEM)],
        out_specs=pl.BlockSpec((1, 2, blk, blk), lambda h: (h, 0, 0, 0)),
        out_shape=jax.ShapeDtypeStruct((H_B, 2, blk, blk), F32),
        compiler_params=_params(("arbitrary",)),
        name="prompt_bias",
    )(rel_bias.astype(F32).reshape(-1))


def _sample_bias_body(tab_ref, o_ref, *, t_new):
    h = pl.program_id(0)
    shape = (2 * t_new, PAGE_SIZE * H_B)
    r = lax.broadcasted_iota(jnp.int32, shape, 0)
    j = lax.broadcasted_iota(jnp.int32, shape, 1)
    tok = jnp.where(r >= t_new, r - t_new, r)
    c = j // H_B
    same_head = (j % H_B) == h
    far = jnp.full(shape, LOG2E * tab_ref[(NUM_BUCKETS - 1) * H_B + h], F32)
    o_ref[0] = jnp.where(same_head, far, NEG_INF)
    o_ref[1] = jnp.where(same_head, LOG2E * _bias_of_distance(PAGE_SIZE + tok - c, tab_ref, h), NEG_INF)
    d = tok - c
    o_ref[2] = jnp.where(same_head & (d >= 0), LOG2E * _bias_of_distance(jnp.maximum(d, 0), tab_ref, h), NEG_INF)


def _sample_bias_tiles(rel_bias, t_new):
    rows = 2 * t_new
    return pl.pallas_call(
        functools.partial(_sample_bias_body, t_new=t_new),
        grid=(H_B,),
        in_specs=[pl.BlockSpec(memory_space=pltpu.SMEM)],
        out_specs=pl.BlockSpec((3, rows, PAGE_SIZE * H_B), lambda h: (0, h, 0)),
        out_shape=jax.ShapeDtypeStruct((3, H_B * rows, PAGE_SIZE * H_B), F32),
        compiler_params=_params(("arbitrary",)),
        name="sample_bias",
    )(rel_bias.astype(F32).reshape(-1))


def _lambda(lam_ref, lam_init):
    lq = lam_ref[...]
    a = jnp.sum(lq[0:1] * lq[1:2], axis=1, keepdims=True)
    b = jnp.sum(lq[2:3] * lq[3:4], axis=1, keepdims=True)
    return jnp.exp(a) - jnp.exp(b) + lam_init


def _stack_maps(q):
    lane = lax.broadcasted_iota(jnp.int32, q.shape, 1)
    q = q * (DK_B ** -0.5 * LOG2E)
    return jnp.concatenate([jnp.where(lane < DK_B, q, 0.0), jnp.where(lane >= DK_B, q, 0.0)], axis=0)


def _attn_prompt_body(far_ref, q_ref, k_ref, vt_ref, bias_ref, lam_ref, sw_ref, o_ref,
                      qt_scr, m_scr, acc_scr, *, blk, hp, lam_init):
    h0 = pl.program_id(1) * hp
    i = pl.program_id(2)
    hd = 2 * DK_B
    for hh in range(hp):
        qt_scr[hh] = _stack_maps(q_ref[0, :, hh * hd:(hh + 1) * hd]).T.astype(BF16)
    m_scr[...] = jnp.full(m_scr.shape, NEG_INF, F32)
    acc_scr[...] = jnp.zeros(acc_scr.shape, F32)
    far = [far_ref[h0 + hh] * LOG2E for hh in range(hp)]
    gw = min(blk, 2 * LANES)
    groups = [(hh, slice(g * gw, (g + 1) * gw)) for hh in range(hp) for g in range(2 * blk // gw)]

    def bias_of(hh, cs, nkeys):
        c0 = cs.start % blk
        if nkeys == 2 * blk:
            return bias_ref[hh, :, :, c0:c0 + gw].reshape(2 * blk, gw)
        return bias_ref[hh, 1, :, c0:c0 + gw]

    def tile(start, nkeys, biased):
        kj = [k_ref[0, pl.ds(start, nkeys), hh * hd:(hh + 1) * hd] for hh in range(hp)]
        vtj = [vt_ref[0, hh, :, pl.ds(start, nkeys)] for hh in range(hp)]
        s = [jnp.dot(kj[hh], qt_scr[hh, :, cs], preferred_element_type=F32) for hh, cs in groups]
        if biased:
            s = [sg + bias_of(hh, cs, nkeys) for sg, (hh, cs) in zip(s, groups)]
        p, alpha, m_new = [], [], []
        for sg, (hh, cs) in zip(s, groups):
            m_old = m_scr[hh, :, cs]
            m_cur = jnp.max(sg, axis=0, keepdims=True)
            if not biased:
                m_cur = m_cur + far[hh]
            mn = jnp.maximum(m_old, m_cur)
            alpha.append(jnp.exp2(m_old - mn))
            p.append(jnp.exp2(sg - (mn if biased else mn - far[hh])).astype(BF16))
            m_new.append(mn)
        pv = [jnp.dot(vtj[hh], pg, preferred_element_type=F32) for pg, (hh, cs) in zip(p, groups)]
        for x, (hh, cs) in enumerate(groups):
            acc_scr[hh, :, cs] = alpha[x] * acc_scr[hh, :, cs] + pv[x]
            m_scr[hh, :, cs] = m_new[x]

    n_far = jnp.maximum(i - 1, 0)

    def far_pair(jj, carry):
        tile(pl.multiple_of(jj * 2 * blk, 2 * blk), 2 * blk, False)
        return carry

    lax.fori_loop(0, n_far // 2, far_pair, 0)

    @pl.when(n_far % 2 == 1)
    def _():
        tile(pl.multiple_of((n_far - 1) * blk, blk), blk, False)

    @pl.when(i >= 1)
    def _():
        tile(pl.multiple_of((i - 1) * blk, blk), 2 * blk, True)

    @pl.when(i == 0)
    def _():
        tile(0, blk, True)

    lam = _lambda(lam_ref, lam_init)
    for hh in range(hp):
        acc = acc_scr[hh]
        wt = acc[:DV_B] / acc[DV_B:DV_B + 1]
        ot = wt[:, :blk] - lam * wt[:, blk:]
        ot = ot * lax.rsqrt(jnp.mean(ot * ot, axis=0, keepdims=True) + 1e-5)
        o_ref[0, :, hh * DV_B:(hh + 1) * DV_B] = (ot.T * sw_ref[...] * (1.0 - lam_init)).astype(o_ref.dtype)


def _attn_prompt(q, k16, vt16, rel_bias, lam_q, subln_w, lam_init, *, blk=512, hp=2):
    B, T, _ = q.shape
    blk = min(blk, T)
    bias = _prompt_bias_tiles(rel_bias, blk)
    far = rel_bias[NUM_BUCKETS - 1].astype(F32)
    vrows = vt16.shape[2]
    return pl.pallas_call(
        functools.partial(_attn_prompt_body, blk=blk, hp=hp, lam_init=lam_init),
        grid=(B, H_B // hp, T // blk),
        in_specs=[pl.BlockSpec(memory_space=pltpu.SMEM),
                  pl.BlockSpec((1, blk, hp * 2 * DK_B), lambda b, h, i: (b, i, h)),
                  pl.BlockSpec((1, T, hp * 2 * DK_B), lambda b, h, i: (b, 0, h)),
                  pl.BlockSpec((1, hp, vrows, T), lambda b, h, i: (b, h, 0, 0)),
                  pl.BlockSpec((hp, 2, blk, blk), lambda b, h, i: (h, 0, 0, 0)),
                  pl.BlockSpec((4, DK_B), lambda b, h, i: (0, 0)),
                  pl.BlockSpec((1, DV_B), lambda b, h, i: (0, 0))],
        out_specs=pl.BlockSpec((1, blk, hp * DV_B), lambda b, h, i: (b, i, h)),
        scratch_shapes=[pltpu.VMEM((hp, 2 * DK_B, 2 * blk), BF16), pltpu.VMEM((hp, 1, 2 * blk), F32),
                        pltpu.VMEM((hp, vrows, 2 * blk), F32)],
        out_shape=jax.ShapeDtypeStruct((B, T, H_B * DV_B), BF16),
        compiler_params=_params(("parallel", "parallel", "arbitrary")),
        name="attn_prompt",
    )(far, q, k16, vt16, bias, lam_q.astype(F32), subln_w.reshape(1, DV_B).astype(F32))


def _attn_sample_body(*refs, pp, t_new, lam_init):
    pt_ref, q_ref = refs[:2]
    k_refs = refs[2:2 + pp]
    v_refs = refs[2 + pp:2 + 2 * pp]
    kn_ref, vn_ref, bias_ref, lam_ref, sw_ref, o_ref, m_scr, l_scr, acc_scr = refs[2 + 2 * pp:]
    step = pl.program_id(1)
    last = step == pl.num_programs(1) - 1
    rows = 2 * t_new
    flat = PAGE_SIZE * H_B

    @pl.when(step == 0)
    def _():
        m_scr[...] = jnp.full(m_scr.shape, NEG_INF, F32)
        l_scr[...] = jnp.zeros(l_scr.shape, F32)
        acc_scr[...] = jnp.zeros(acc_scr.shape, F32)

    qall = jnp.concatenate([_stack_maps(q_ref[0, :, h * 2 * DK_B:(h + 1) * 2 * DK_B]) for h in range(H_B)],
                           axis=0).astype(BF16)

    def update(scores, values):
        m_old = m_scr[...]
        m_cur = jnp.max(scores[0], axis=1, keepdims=True)
        for s in scores[1:]:
            m_cur = jnp.maximum(m_cur, jnp.max(s, axis=1, keepdims=True))
        m_new = jnp.maximum(m_old, m_cur)
        alpha = jnp.exp2(m_old - m_new)
        l_new = alpha * l_scr[...]
        pv = None
        for s, v in zip(scores, values):
            p = jnp.exp2(s - m_new)
            l_new = l_new + jnp.sum(p, axis=1, keepdims=True)
            part = jnp.dot(p.astype(BF16), v, preferred_element_type=F32)
            pv = part if pv is None else pv + part
        l_scr[...] = l_new
        acc_scr[...] = alpha * acc_scr[...] + pv
        m_scr[...] = m_new

    def flat_page(ref):
        return ref[0].reshape(flat, 2 * DK_B).astype(BF16)

    scores = []
    for r in range(pp):
        bias = bias_ref[last.astype(jnp.int32)] if r == pp - 1 else bias_ref[0]
        scores.append(_dot_nt(qall, flat_page(k_refs[r])) + bias)
    update(scores, [flat_page(v_refs[r]) for r in range(pp)])

    @pl.when(last)
    def _():
        update([_dot_nt(qall, flat_page(kn_ref)) + bias_ref[2]], [flat_page(vn_ref)])
        w = acc_scr[...] / l_scr[...]
        lam = _lambda(lam_ref, lam_init)
        for h in range(H_B):
            o = w[h * rows:h * rows + t_new] - lam * w[h * rows + t_new:(h + 1) * rows]
            o = o * lax.rsqrt(jnp.mean(o * o, -1, keepdims=True) + 1e-5) * sw_ref[...] * (1.0 - lam_init)
            o_ref[0, :, h * DV_B:(h + 1) * DV_B] = o.astype(o_ref.dtype)


def _attn_sample(q, k_new, v_new, cache_k, cache_v, page_table, rel_bias, lam_q, subln_w, lam_init, *, pp=8):
    Bd, t_new, _ = q.shape
    n_pages = page_table.shape[1]
    assert n_pages % pp == 0
    bias = _sample_bias_tiles(rel_bias, t_new)
    padt = lambda a: jnp.pad(a, ((0, 0), (0, PAGE_SIZE - t_new), (0, 0), (0, 0)))
    page_spec = lambda r: pl.BlockSpec((1, PAGE_SIZE, H_B, 2 * DK_B),
                                       lambda b, s, pt, r=r: (pt[b, s * pp + r], 0, 0, 0))
    own_spec = pl.BlockSpec((1, PAGE_SIZE, H_B, 2 * DK_B), lambda b, s, pt: (b, 0, 0, 0))
    const = lambda *shape: pl.BlockSpec(shape, lambda b, s, pt: (0,) * len(shape))
    rows = H_B * 2 * t_new
    grid_spec = pltpu.PrefetchScalarGridSpec(
        num_scalar_prefetch=1,
        grid=(Bd, n_pages // pp),
        in_specs=([pl.BlockSpec((1, t_new, H_B * 2 * DK_B), lambda b, s, pt: (b, 0, 0))]
                  + [page_spec(r) for r in range(pp)] + [page_spec(r) for r in range(pp)]
                  + [own_spec, own_spec, const(3, rows, PAGE_SIZE * H_B), const(4, DK_B), const(1, DV_B)]),
        out_specs=pl.BlockSpec((1, t_new, H_B * DV_B), lambda b, s, pt: (b, 0, 0)),
        scratch_shapes=[pltpu.VMEM((rows, 1), F32), pltpu.VMEM((rows, 1), F32), pltpu.VMEM((rows, DV_B), F32)],
    )
    return pl.pallas_call(
        functools.partial(_attn_sample_body, pp=pp, t_new=t_new, lam_init=lam_init),
        grid_spec=grid_spec,
        out_shape=jax.ShapeDtypeStruct((Bd, t_new, H_B * DV_B), BF16),
        compiler_params=_params(("parallel", "arbitrary")),
        name="attn_sample",
    )(page_table, q, *([cache_k] * pp), *([cache_v] * pp), padt(k_new), padt(v_new), bias,
      lam_q.astype(F32), subln_w.reshape(1, DV_B).astype(F32))


def _gdn_mixer(x2d, B, T, w_qkv, w_zba, conv_w, a_log, dt_bias, gnorm_w, conv_state, ssm_state):
    state8 = jnp.pad(conv_state.astype(F32), ((0, 0), (8 - (CONV_W - 1), 0), (0, 0)))
    fused_tile = 512
    if T % fused_tile == 0:
        qkvc, tail8 = _inproj_conv(x2d, w_qkv, state8, conv_w.astype(F32), T, tm=fused_tile)
        qkvc = qkvc.reshape(B, T, CONV_DIM)
        xpad_tail = tail8[:, 8 - (CONV_W - 1):]
    else:
        proj = _mm(x2d, w_qkv).reshape(B, T, CONV_DIM)
        qkvc = _gdn_prep(proj, state8, conv_w.astype(F32), tt=fused_tile)
        xpad_tail = jnp.concatenate([conv_state.astype(F32), proj], axis=1)[:, -(CONV_W - 1):]
    zba2d = _mm(x2d, w_zba, tm=512)
    proj = zba2d.reshape(B, T, -1)
    bgc = _gdn_gates(zba2d, VDIM_A // LANES, a_log, dt_bias, min(CHUNK, T)).reshape(B, T, LANES)
    if T % CHUNK:
        tp = -(-T // CHUNK) * CHUNK
        padt = lambda a: jnp.pad(a, ((0, 0), (0, tp - T), (0, 0)))
        lane = jnp.arange(LANES)
        is_gc = (lane >= 2 * H_V) & (lane < 3 * H_V)
        tail = jnp.broadcast_to(jnp.where(is_gc, bgc[:, T - 1:T], 0.0), (B, tp - T, LANES))
        bgc = jnp.concatenate([bgc, tail], axis=1)
        qkvc = padt(qkvc)
        z_arr = padt(proj[:, :, :VDIM_A])
    else:
        z_arr = proj
    u, w, qg, kd, aqk, dec = _gdn_intra(qkvc, bgc)
    y, s_new = _gdn_state(u, w, qg, kd, aqk, dec, z_arr, 0, ssm_state, gnorm_w)
    return y[:, :T].reshape(B * T, VDIM_A), xpad_tail, s_new


def _mlp_and_embed(x, p2d, ln_g, ln_b, w_up, w_down, w_ple, w_pg):
    u = _mm(x, w_up, act="relu2", out_dtypes=(BF16,), tn=1024)
    x = _mm_ln(u, w_down, x, ln_g, ln_b)
    return _ple_gate(x, p2d, w_ple, w_pg)


def kernel(x_prompt, x_sample, state_conv, state_ssm, cache_k, cache_v, page_table, p_prompt, p_sample,
           w_in_a, conv_w_a, a_log_a, dt_bias_a, gnorm_w_a, w_out_a, w_kv, w_q_b, lam_b, subln_w_b, w_o_b,
           rel_bias, ln_g, ln_b, w_up, w_down, w_ple, w_pg):
    bf = lambda w: w.astype(BF16)
    w_qkv = bf(w_in_a[:, :, :CONV_DIM])
    w_zba = bf(jnp.pad(w_in_a[:, :, CONV_DIM:], ((0, 0), (0, 0), (0, IN_A_PAD - IN_A))))
    w_out, w_kv16, w_q, w_o = bf(w_out_a), bf(w_kv), bf(w_q_b), bf(w_o_b)
    w_up16, w_down16, w_ple16, w_pg16 = bf(w_up), bf(w_down), bf(w_ple), bf(w_pg)

    def run(x3d, p, conv_state, ssm_state, attn_fn):
        B, T, _ = x3d.shape
        x = x3d.reshape(B * T, D_MODEL)
        convs, ssms = [], []
        kv = kv16 = None
        for i in range(DEPTH):
            if i < N_A:
                y, cs, ss = _gdn_mixer(x, B, T, w_qkv[i], w_zba[i], conv_w_a[i], a_log_a[i], dt_bias_a[i],
                                       gnorm_w_a[i], conv_state[i], ssm_state[i])
                convs.append(cs)
                ssms.append(ss)
                x = _mm_ln(y, w_out[i], x, ln_g[i, 0], ln_b[i, 0])
            else:
                j = i - N_A
                lam_init = 0.8 - 0.6 * math.exp(-0.3 * i)
                q = _mm(x, w_q[j]).reshape(B, T, H_B * 2 * DK_B)
                o = attn_fn(q, kv, kv16, lam_b[j], subln_w_b[j], lam_init)
                x = _mm_ln(o.reshape(B * T, H_B * DV_B), w_o[j], x, ln_g[i, 0], ln_b[i, 0])
            x = _mlp_and_embed(x, p[i].reshape(B * T, -1), ln_g[i, 1], ln_b[i, 1], w_up16[i], w_down16[i],
                               w_ple16[i], w_pg16[i])
            if i == N_A - 1:
                kv, kv16 = _mm(x, w_kv16, out_dtypes=(F32, BF16), tn=1024)
                kv, kv16 = kv.reshape(B, T, -1), kv16.reshape(B, T, -1)
        k_sh = kv[..., :H_B * 2 * DK_B].reshape(B, T, H_B, 2 * DK_B)
        v_sh = kv[..., H_B * 2 * DK_B:].reshape(B, T, H_B, DV_B)
        return x.reshape(B, T, D_MODEL), jnp.stack(convs), jnp.stack(ssms), k_sh, v_sh

    def attn_prompt(q, kv, kv16, lam_q, sw, lam_init):
        B, T, _ = q.shape
        vt16 = kv16[..., H_B * 2 * DK_B:].reshape(B, T, H_B, DV_B).transpose(0, 2, 3, 1)
        ones = jnp.ones((B, H_B, 16, T), BF16)
        return _attn_prompt(q, kv16, jnp.concatenate([vt16, ones], axis=2), rel_bias, lam_q, sw, lam_init)

    def attn_sample(q, kv, kv16, lam_q, sw, lam_init):
        B, T, _ = q.shape
        k_new = kv[..., :H_B * 2 * DK_B].reshape(B, T, H_B, 2 * DK_B)
        v_new = kv[..., H_B * 2 * DK_B:].reshape(B, T, H_B, DV_B)
        return _attn_sample(q, k_new, v_new, cache_k, cache_v, page_table, rel_bias, lam_q, sw, lam_init)

    Bp = x_prompt.shape[0]
    conv0 = jnp.zeros((N_A, Bp, CONV_W - 1, CONV_DIM), x_prompt.dtype)
    ssm0 = jnp.zeros((N_A, Bp, H_V, DK_A, DV_A), state_ssm.dtype)
    y_prompt, conv_p, ssm_p, k_p, v_p = run(x_prompt, p_prompt, conv0, ssm0, attn_prompt)
    y_sample, conv_s, ssm_s, k_s, v_s = run(x_sample, p_sample, state_conv, state_ssm, attn_sample)
    return (y_prompt, y_sample, conv_p, ssm_p, k_p, v_p, conv_s, ssm_s, k_s, v_s)
```

```python
import functools
import math

import jax
import jax.numpy as jnp
from jax import lax
from jax.experimental import pallas as pl
from jax.experimental.pallas import tpu as pltpu

F32 = jnp.float32
BF16 = jnp.bfloat16

D_MODEL = 1024
DEPTH = 2
N_A = DEPTH // 2
PAGE_SIZE = 128
H_QK = 8
H_V = 16
DK_A = 128
DV_A = 128
CONV_W = 4
CHUNK = 64
QK_DIM = H_QK * DK_A
VDIM_A = H_V * DV_A
CONV_DIM = 2 * QK_DIM + VDIM_A
IN_A = CONV_DIM + VDIM_A + 2 * H_V
IN_A_PAD = CONV_DIM + VDIM_A + 128
H_B = 8
DK_B = 64
DV_B = 128
NUM_BUCKETS = 32
MAX_DISTANCE = 128
D_FF = 4 * D_MODEL
ALPHA = (2 * DEPTH) ** 0.25

LANES = 128
VMEM_LIMIT = 56 * 1024 * 1024
NEG_INF = float("-inf")
LOG2E = math.log2(math.e)


def _params(sem, vmem=VMEM_LIMIT):
    return pltpu.CompilerParams(dimension_semantics=sem, vmem_limit_bytes=vmem)


def _sigmoid(x):
    return 1.0 / (1.0 + jnp.exp(-x))


def _dot_nt(a, b):
    return lax.dot_general(a, b, (((1,), (1,)), ((), ())), preferred_element_type=F32)


def _dot_tn(a, b):
    return lax.dot_general(a, b, (((0,), (0,)), ((), ())), preferred_element_type=F32)


def _mm_body(x_ref, w_ref, *o_refs, act):
    acc = jnp.dot(x_ref[...].astype(BF16), w_ref[...], preferred_element_type=F32)
    if act == "relu2":
        r = jnp.maximum(acc, 0.0)
        acc = r * r
    for o_ref in o_refs:
        o_ref[...] = acc.astype(o_ref.dtype)


def _mm(x, w, *, act=None, out_dtypes=(F32,), tm=1024, tn=None):
    M, K = x.shape
    N = w.shape[1]
    tm = min(tm, M)
    tn = N if tn is None else tn
    assert M % tm == 0 and N % tn == 0
    outs = pl.pallas_call(
        functools.partial(_mm_body, act=act),
        grid=(M // tm, N // tn),
        in_specs=[pl.BlockSpec((tm, K), lambda i, j: (i, 0)),
                  pl.BlockSpec((K, tn), lambda i, j: (0, j))],
        out_specs=[pl.BlockSpec((tm, tn), lambda i, j: (i, j)) for _ in out_dtypes],
        out_shape=[jax.ShapeDtypeStruct((M, N), dt) for dt in out_dtypes],
        compiler_params=_params(("parallel", "arbitrary")),
        name="mm",
    )(x, w)
    return outs[0] if len(out_dtypes) == 1 else outs


def _mm_ln_body(x_ref, w_ref, r_ref, g_ref, b_ref, o_ref):
    acc = jnp.dot(x_ref[...].astype(BF16), w_ref[...], preferred_element_type=F32)
    y = ALPHA * r_ref[...] + acc
    mu = jnp.mean(y, -1, keepdims=True)
    yc = y - mu
    var = jnp.mean(yc * yc, -1, keepdims=True)
    o_ref[...] = yc * lax.rsqrt(var + 1e-5) * g_ref[...] + b_ref[...]


def _mm_ln(x, w, res, g, b, *, tm=512):
    M, K = x.shape
    N = w.shape[1]
    tm = min(tm, M)
    return pl.pallas_call(
        _mm_ln_body,
        grid=(M // tm,),
        in_specs=[pl.BlockSpec((tm, K), lambda i: (i, 0)),
                  pl.BlockSpec((K, N), lambda i: (0, 0)),
                  pl.BlockSpec((tm, N), lambda i: (i, 0)),
                  pl.BlockSpec((1, N), lambda i: (0, 0)),
                  pl.BlockSpec((1, N), lambda i: (0, 0))],
        out_specs=pl.BlockSpec((tm, N), lambda i: (i, 0)),
        out_shape=jax.ShapeDtypeStruct((M, N), F32),
        compiler_params=_params(("parallel",)),
        name="mm_ln",
    )(x, w, res, g.reshape(1, N), b.reshape(1, N))


def _layernorm_rows(y, g, b):
    mu = jnp.mean(y, -1, keepdims=True)
    yc = y - mu
    var = jnp.mean(yc * yc, -1, keepdims=True)
    return yc * lax.rsqrt(var + 1e-5) * g + b


def _mlp_gate_body(x_ref, p_ref, wup_ref, wdown_ref, g_ref, b_ref, wple_ref, wpg_ref, o_ref):
    x = x_ref[...]
    u = jnp.maximum(jnp.dot(x.astype(BF16), wup_ref[...], preferred_element_type=F32), 0.0)
    h = (u * u).astype(BF16)
    x2 = _layernorm_rows(ALPHA * x + jnp.dot(h, wdown_ref[...], preferred_element_type=F32), g_ref[...], b_ref[...])
    e = jnp.dot(p_ref[...].astype(BF16), wple_ref[...], preferred_element_type=F32)
    gt = jnp.dot(x2.astype(BF16), wpg_ref[...], preferred_element_type=F32)
    o_ref[...] = x2 + e * _sigmoid(gt)


def _mlp_gate(x, p, w_up, w_down, g, b, w_ple, w_pg, *, tm=512):
    M, D = x.shape
    P = p.shape[1]
    F = w_up.shape[1]
    tm = min(tm, M)
    resident = lambda r, c: pl.BlockSpec((r, c), lambda i: (0, 0), pipeline_mode=pl.Buffered(1))
    return pl.pallas_call(
        _mlp_gate_body,
        grid=(M // tm,),
        in_specs=[pl.BlockSpec((tm, D), lambda i: (i, 0)),
                  pl.BlockSpec((tm, P), lambda i: (i, 0)),
                  resident(D, F), resident(F, D), resident(1, D), resident(1, D), resident(P, D), resident(D, D)],
        out_specs=pl.BlockSpec((tm, D), lambda i: (i, 0)),
        out_shape=jax.ShapeDtypeStruct((M, D), F32),
        compiler_params=_params(("parallel",)),
        name="mlp_gate",
    )(x, p, w_up, w_down, g.reshape(1, D), b.reshape(1, D), w_ple, w_pg)


def _conv_silu_norm(prev, x, cw_ref, store, first_group):
    xf = jnp.concatenate([prev, x], axis=0)
    for h in range(x.shape[1] // LANES):
        sl = slice(h * LANES, (h + 1) * LANES)
        xs = xf[:, sl]
        cw = cw_ref[:, sl]
        c = xs * cw[3:4]
        for s in (1, 2, 3):
            c = c + pltpu.roll(xs, s, 0) * cw[3 - s:4 - s]
        c = c[8:]
        c = c * _sigmoid(c)
        group = first_group + h
        nrm = lax.rsqrt(jnp.sum(c * c, -1, keepdims=True) + 1e-6)
        scale = jnp.where(group < H_QK, nrm * (DK_A ** -0.5), jnp.where(group < 2 * H_QK, nrm, 1.0))
        store(sl, c * scale)


def _gdn_prep_body(x_ref, halo_ref, st_ref, cw_ref, o_ref, *, gw, nb):
    t = pl.program_id(1)
    j = pl.program_id(2)
    for bi in range(nb):
        prev = jnp.where(t == 0, st_ref[bi], halo_ref[bi])

        def store(sl, val, bi=bi):
            o_ref[bi, :, sl] = val

        _conv_silu_norm(prev, x_ref[bi], cw_ref, store, j * (gw // LANES))


def _inproj_conv_body(x_ref, w_ref, st_ref, cw_ref, o_ref, tail_ref, carry_scr, *, tiles_per_seq):
    i = pl.program_id(0)
    j = pl.program_id(1)
    tn = w_ref.shape[1]
    acc = jnp.dot(x_ref[...].astype(BF16), w_ref[...], preferred_element_type=F32)
    prev = jnp.where(i % tiles_per_seq == 0, st_ref[0], carry_scr[j])

    def store(sl, val):
        o_ref[:, sl] = val

    _conv_silu_norm(prev, acc, cw_ref, store, j * (tn // LANES))
    carry_scr[j] = acc[acc.shape[0] - 8:]
    tail_ref[0] = acc[acc.shape[0] - 8:]


def _inproj_conv(x2d, w_qkv, state8, conv_w, T, *, tm=512, tn=1024):
    M, K = x2d.shape
    assert T % tm == 0 and CONV_DIM % tn == 0
    tps = T // tm
    qkvc, tails = pl.pallas_call(
        functools.partial(_inproj_conv_body, tiles_per_seq=tps),
        grid=(M // tm, CONV_DIM // tn),
        in_specs=[pl.BlockSpec((tm, K), lambda i, j: (i, 0)),
                  pl.BlockSpec((K, tn), lambda i, j: (0, j)),
                  pl.BlockSpec((1, 8, tn), lambda i, j: (i // tps, 0, j)),
                  pl.BlockSpec((CONV_W, tn), lambda i, j: (0, j))],
        out_specs=[pl.BlockSpec((tm, tn), lambda i, j: (i, j)),
                   pl.BlockSpec((1, 8, tn), lambda i, j: (i, 0, j))],
        out_shape=[jax.ShapeDtypeStruct((M, CONV_DIM), F32), jax.ShapeDtypeStruct((M // tm, 8, CONV_DIM), F32)],
        scratch_shapes=[pltpu.VMEM((CONV_DIM // tn, 8, tn), F32)],
        compiler_params=_params(("arbitrary", "arbitrary")),
        name="inproj_conv",
    )(x2d, w_qkv, state8, conv_w)
    return qkvc, tails[tps - 1::tps]


def _gdn_prep(proj, state8, conv_w, *, tt, gw=512):
    B, T, _ = proj.shape
    tt = min(tt, T)
    hb = tt // 8
    nb = min(max(64 // tt, 1), B)
    assert B % nb == 0
    return pl.pallas_call(
        functools.partial(_gdn_prep_body, gw=gw, nb=nb),
        grid=(B // nb, T // tt, CONV_DIM // gw),
        in_specs=[pl.BlockSpec((nb, tt, gw), lambda b, t, j: (b, t, j)),
                  pl.BlockSpec((nb, 8, gw), lambda b, t, j: (b, jnp.maximum(t * hb - 1, 0), j)),
                  pl.BlockSpec((nb, 8, gw), lambda b, t, j: (b, 0, j)),
                  pl.BlockSpec((CONV_W, gw), lambda b, t, j: (0, j))],
        out_specs=pl.BlockSpec((nb, tt, gw), lambda b, t, j: (b, t, j)),
        out_shape=jax.ShapeDtypeStruct((B, T, CONV_DIM), F32),
        compiler_params=_params(("parallel", "parallel", "parallel")),
        name="gdn_prep",
    )(proj, proj, state8, conv_w)


def _gdn_gates_body(x_ref, alog_ref, dtb_ref, o_ref, *, period):
    ba = x_ref[...]
    lane = lax.broadcasted_iota(jnp.int32, ba.shape, 1)
    beta = _sigmoid(ba)
    xx = ba + dtb_ref[...]
    softplus = jnp.maximum(xx, 0.0) + jnp.log(1.0 + jnp.exp(-jnp.abs(xx)))
    g = -jnp.exp(alog_ref[...]) * softplus
    pos = lax.broadcasted_iota(jnp.int32, ba.shape, 0) % period
    gc = g
    s = 1
    while s < period:
        gc = gc + jnp.where(pos >= s, pltpu.roll(gc, s, 0), 0.0)
        s *= 2
    gc = pltpu.roll(gc, H_V, 1)
    o_ref[...] = jnp.where(lane < H_V, beta,
                           jnp.where(lane < 2 * H_V, g, jnp.where(lane < 3 * H_V, gc, 0.0)))


def _gdn_gates(proj2d, col, a_log, dt_bias, period, *, tm=1024):
    M = proj2d.shape[0]
    tm = min(tm, M)
    assert tm % period == 0
    pad = lambda v: jnp.pad(v.astype(F32), (H_V, LANES - 2 * H_V)).reshape(1, LANES)
    return pl.pallas_call(
        functools.partial(_gdn_gates_body, period=period),
        grid=(M // tm,),
        in_specs=[pl.BlockSpec((tm, LANES), lambda i: (i, col)),
                  pl.BlockSpec((1, LANES), lambda i: (0, 0)),
                  pl.BlockSpec((1, LANES), lambda i: (0, 0))],
        out_specs=pl.BlockSpec((tm, LANES), lambda i: (i, 0)),
        out_shape=jax.ShapeDtypeStruct((M, LANES), F32),
        compiler_params=_params(("parallel",)),
        name="gdn_gates",
    )(proj2d, pad(a_log), pad(dt_bias))


def _gdn_intra_body(q_ref, k_ref, v_ref, bgc_ref, gct_ref, u_ref, w_ref, qg_ref, kd_ref, aqk_ref, dec_ref,
                    *, nb, ca):
    hk = pl.program_id(2)
    C = CHUNK
    row = lax.broadcasted_iota(jnp.int32, (C, C), 0)
    col = lax.broadcasted_iota(jnp.int32, (C, C), 1)
    incl = row >= col
    strict = row > col
    eye = (row == col).astype(F32)
    lane = lax.broadcasted_iota(jnp.int32, (C, LANES), 1)
    blocks = [(bi, ci) for bi in range(nb) for ci in range(ca)]
    chains = [(x, e) for x in range(len(blocks)) for e in range(2)]
    at = lambda x: (blocks[x][0], slice(blocks[x][1] * C, (blocks[x][1] + 1) * C))
    full = (slice(None),)
    k16 = [k_ref[at(x) + full].astype(BF16) for x in range(len(blocks))]
    kk = [_dot_nt(a, a) for a in k16]
    qk = [_dot_nt(q_ref[at(x) + full].astype(BF16), k16[x]) for x in range(len(blocks))]
    beta, gc, npow = [], [], []
    for x, e in chains:
        bi, ci = blocks[x]
        hv = 2 * hk + e
        bgc = bgc_ref[at(x) + full]
        b = jnp.sum(jnp.where(lane == hv, bgc, 0.0), axis=1, keepdims=True)
        g = jnp.sum(jnp.where(lane == 2 * H_V + hv, bgc, 0.0), axis=1, keepdims=True)
        dm = jnp.exp(jnp.where(incl, g - gct_ref[bi, ci, pl.ds(hv, 1), :], NEG_INF))
        aqk_ref[at(x) + (slice(e * C, (e + 1) * C),)] = jnp.where(incl, qk[x] * dm, 0.0).astype(BF16)
        npow.append(jnp.where(strict, -(b * kk[x] * dm), 0.0))
        beta.append(b)
        gc.append(g)
    tinv = [eye + n for n in npow]
    for _ in range(5):
        n16 = [n.astype(BF16) for n in npow]
        npow = [jnp.dot(a, a, preferred_element_type=F32) for a in n16]
        tinv = [t + jnp.dot(t.astype(BF16), n.astype(BF16), preferred_element_type=F32)
                for t, n in zip(tinv, npow)]
    eg = [jnp.exp(g) for g in gc]
    rhs = []
    for y, (x, e) in enumerate(chains):
        k = k_ref[at(x) + full]
        v = v_ref[at(x) + (slice(e * DV_A, (e + 1) * DV_A),)]
        rhs.append(jnp.concatenate([v * beta[y], k * (beta[y] * eg[y])], axis=1).astype(BF16))
    uw = [jnp.dot(t.astype(BF16), r, preferred_element_type=F32) for t, r in zip(tinv, rhs)]
    for y, (x, e) in enumerate(chains):
        bi, ci = blocks[x]
        hs = at(x) + (slice(e * DV_A, (e + 1) * DV_A),)
        u_ref[hs] = uw[y][:, :DV_A].astype(BF16)
        w_ref[hs] = uw[y][:, DV_A:].astype(BF16)
        qg_ref[hs] = (q_ref[at(x) + full] * eg[y]).astype(BF16)
        g_last = gc[y][C - 1:C, :]
        kd_ref[hs] = (k_ref[at(x) + full] * jnp.exp(g_last - gc[y])).astype(BF16)
        dec_ref[bi, 0, ci, e:e + 1, :] = jnp.broadcast_to(jnp.exp(g_last), (1, LANES))


def _gdn_intra(qkvc, bgc, *, chains=32):
    B, T, _ = qkvc.shape
    nc = T // CHUNK
    ca = min(chains // 2, nc)
    nb = min(max(chains // (2 * ca), 1), B)
    assert nc % ca == 0 and B % nb == 0
    ta = ca * CHUNK
    gct = bgc[..., 2 * H_V:3 * H_V].reshape(B, nc, CHUNK, H_V).transpose(0, 1, 3, 2)
    kb = QK_DIM // DK_A
    wide = lambda: pl.BlockSpec((nb, ta, 2 * DV_A), lambda b, t, h: (b, t, h))
    return pl.pallas_call(
        functools.partial(_gdn_intra_body, nb=nb, ca=ca),
        grid=(B // nb, T // ta, H_QK),
        in_specs=[pl.BlockSpec((nb, ta, DK_A), lambda b, t, h: (b, t, h)),
                  pl.BlockSpec((nb, ta, DK_A), lambda b, t, h: (b, t, kb + h)),
                  pl.BlockSpec((nb, ta, 2 * DV_A), lambda b, t, h: (b, t, kb + h)),
                  pl.BlockSpec((nb, ta, LANES), lambda b, t, h: (b, t, 0)),
                  pl.BlockSpec((nb, ca, H_V, CHUNK), lambda b, t, h: (b, t, 0, 0))],
        out_specs=[wide(), wide(), wide(), wide(),
                   pl.BlockSpec((nb, ta, 2 * CHUNK), lambda b, t, h: (b, t, h)),
                   pl.BlockSpec((nb, 1, ca, 2, LANES), lambda b, t, h: (b, h, t, 0, 0))],
        out_shape=[jax.ShapeDtypeStruct((B, T, VDIM_A), BF16)] * 4
        + [jax.ShapeDtypeStruct((B, T, H_V * CHUNK), BF16),
           jax.ShapeDtypeStruct((B, H_QK, nc, 2, LANES), F32)],
        compiler_params=_params(("parallel", "parallel", "arbitrary")),
        name="gdn_intra",
    )(qkvc, qkvc, qkvc, bgc, gct)


def _gdn_state_body(u_ref, w_ref, qg_ref, kd_ref, aqk_ref, dec_ref, z_ref, s0_ref, gw_ref, y_ref, st_ref, s_scr,
                    *, cb, hg):
    t = pl.program_id(2)
    C = CHUNK

    @pl.when(t == 0)
    def _():
        s_scr[...] = s0_ref[0]

    def chunk(ci, carry):
        rows = pl.ds(pl.multiple_of(ci * C, C), C)
        cols = [slice(hh * DV_A, (hh + 1) * DV_A) for hh in range(hg)]
        s = [s_scr[hh] for hh in range(hg)]
        r = [jnp.dot(jnp.concatenate([w_ref[0, rows, cs], qg_ref[0, rows, cs]], axis=0), s[hh].astype(BF16),
                     preferred_element_type=F32) for hh, cs in enumerate(cols)]
        v_new = [(u_ref[0, rows, cs].astype(F32) - r[hh][:C]).astype(BF16) for hh, cs in enumerate(cols)]
        s_upd = [_dot_tn(kd_ref[0, rows, cs], v_new[hh]) for hh, cs in enumerate(cols)]
        o_in = [jnp.dot(aqk_ref[0, rows, hh * C:(hh + 1) * C], v_new[hh], preferred_element_type=F32)
                for hh in range(hg)]
        for hh, cs in enumerate(cols):
            dec = dec_ref[0, hh // 2, pl.ds(ci, 1), hh % 2, :]
            s_scr[hh] = s[hh] * dec + s_upd[hh]
            o = r[hh][C:] + o_in[hh]
            z = z_ref[0, rows, cs]
            y = o * lax.rsqrt(jnp.mean(o * o, -1, keepdims=True) + 1e-6) * gw_ref[...] * (z * _sigmoid(z))
            y_ref[0, rows, cs] = y.astype(y_ref.dtype)
        return carry

    lax.fori_loop(0, cb, chunk, 0)

    @pl.when(t == pl.num_programs(2) - 1)
    def _():
        st_ref[0] = s_scr[...]


def _gdn_state(u, w, qg, kd, aqk, dec, z_arr, z_off, s0, gnorm_w, *, cb=8, hg=16):
    B, T, _ = u.shape
    nc = T // CHUNK
    cb = min(cb, nc)
    tb = cb * CHUNK
    ng = H_V // hg
    assert z_off % (hg * DV_A) == 0
    z_col = z_off // (hg * DV_A)
    wide = lambda: pl.BlockSpec((1, tb, hg * DV_A), lambda b, g, t: (b, t, g))
    return pl.pallas_call(
        functools.partial(_gdn_state_body, cb=cb, hg=hg),
        grid=(B, ng, T // tb),
        in_specs=[wide(), wide(), wide(), wide(),
                  pl.BlockSpec((1, tb, hg * CHUNK), lambda b, g, t: (b, t, g)),
                  pl.BlockSpec((1, hg // 2, cb, 2, LANES), lambda b, g, t: (b, g, t, 0, 0)),
                  pl.BlockSpec((1, tb, hg * DV_A), lambda b, g, t: (b, t, z_col + g)),
                  pl.BlockSpec((1, hg, DK_A, DV_A), lambda b, g, t: (b, g, 0, 0)),
                  pl.BlockSpec((1, DV_A), lambda b, g, t: (0, 0))],
        out_specs=[wide(), pl.BlockSpec((1, hg, DK_A, DV_A), lambda b, g, t: (b, g, 0, 0))],
        out_shape=[jax.ShapeDtypeStruct((B, T, VDIM_A), BF16),
                   jax.ShapeDtypeStruct((B, H_V, DK_A, DV_A), F32)],
        scratch_shapes=[pltpu.VMEM((hg, DK_A, DV_A), F32)],
        compiler_params=_params(("parallel", "parallel", "arbitrary")),
        name="gdn_state",
    )(u, w, qg, kd, aqk, dec, z_arr, s0, gnorm_w.reshape(1, DV_A).astype(F32))


def _bucket_starts():
    max_exact = NUM_BUCKETS // 2
    starts = list(range(max_exact + 1))
    n = max_exact
    for b in range(max_exact + 1, NUM_BUCKETS):
        while max_exact + int(math.log(n / max_exact) / math.log(MAX_DISTANCE / max_exact)
                              * (NUM_BUCKETS - max_exact)) < b:
            n += 1
        starts.append(n)
    return starts


_BUCKET_START = _bucket_starts()


def _bias_of_distance(n, tab_ref, h):
    val = jnp.full(n.shape, tab_ref[(NUM_BUCKETS - 1) * H_B + h], F32)
    for b in range(NUM_BUCKETS - 2, -1, -1):
        val = jnp.where(n < _BUCKET_START[b + 1], tab_ref[b * H_B + h], val)
    return val


def _prompt_bias_body(tab_ref, o_ref, *, blk):
    h = pl.program_id(0)
    key = lax.broadcasted_iota(jnp.int32, (blk, blk), 0)
    qry = lax.broadcasted_iota(jnp.int32, (blk, blk), 1)
    d = qry - key
    o_ref[0, 0] = LOG2E * _bias_of_distance(d + blk, tab_ref, h)
    o_ref[0, 1] = jnp.where(d >= 0, LOG2E * _bias_of_distance(jnp.maximum(d, 0), tab_ref, h), NEG_INF)


def _prompt_bias_tiles(rel_bias, blk):
    assert blk >= _BUCKET_START[NUM_BUCKETS - 1]
    return pl.pallas_call(
        functools.partial(_prompt_bias_body, blk=blk),
        grid=(H_B,),
        in_specs=[pl.BlockSpec(memory_space=pltpu.SMEM)],
        out_specs=pl.BlockSpec((1, 2, blk, blk), lambda h: (h, 0, 0, 0)),
        out_shape=jax.ShapeDtypeStruct((H_B, 2, blk, blk), F32),
        compiler_params=_params(("arbitrary",)),
        name="prompt_bias",
    )(rel_bias.astype(F32).reshape(-1))


def _sample_bias_body(tab_ref, o_ref, *, t_new):
    h = pl.program_id(0)
    shape = (2 * t_new, PAGE_SIZE * H_B)
    r = lax.broadcasted_iota(jnp.int32, shape, 0)
    j = lax.broadcasted_iota(jnp.int32, shape, 1)
    tok = jnp.where(r >= t_new, r - t_new, r)
    c = j // H_B
    same_head = (j % H_B) == h
    far = jnp.full(shape, LOG2E * tab_ref[(NUM_BUCKETS - 1) * H_B + h], F32)
    o_ref[0] = jnp.where(same_head, far, NEG_INF)
    o_ref[1] = jnp.where(same_head, LOG2E * _bias_of_distance(PAGE_SIZE + tok - c, tab_ref, h), NEG_INF)
    d = tok - c
    o_ref[2] = jnp.where(same_head & (d >= 0), LOG2E * _bias_of_distance(jnp.maximum(d, 0), tab_ref, h), NEG_INF)


def _sample_bias_tiles(rel_bias, t_new):
    rows = 2 * t_new
    return pl.pallas_call(
        functools.partial(_sample_bias_body, t_new=t_new),
        grid=(H_B,),
        in_specs=[pl.BlockSpec(memory_space=pltpu.SMEM)],
        out_specs=pl.BlockSpec((3, rows, PAGE_SIZE * H_B), lambda h: (0, h, 0)),
        out_shape=jax.ShapeDtypeStruct((3, H_B * rows, PAGE_SIZE * H_B), F32),
        compiler_params=_params(("arbitrary",)),
        name="sample_bias",
    )(rel_bias.astype(F32).reshape(-1))


def _lambda(lam_ref, lam_init):
    lq = lam_ref[...]
    a = jnp.sum(lq[0:1] * lq[1:2], axis=1, keepdims=True)
    b = jnp.sum(lq[2:3] * lq[3:4], axis=1, keepdims=True)
    return jnp.exp(a) - jnp.exp(b) + lam_init


def _stack_maps(q):
    lane = lax.broadcasted_iota(jnp.int32, q.shape, 1)
    q = q * (DK_B ** -0.5 * LOG2E)
    return jnp.concatenate([jnp.where(lane < DK_B, q, 0.0), jnp.where(lane >= DK_B, q, 0.0)], axis=0)


def _attn_prompt_body(far_ref, q_ref, k_ref, vt_ref, bias_ref, lam_ref, sw_ref, o_ref,
                      qt_scr, m_scr, acc_scr, *, blk, hp, lam_init):
    h0 = pl.program_id(1) * hp
    i = pl.program_id(2)
    hd = 2 * DK_B
    for hh in range(hp):
        qt_scr[hh] = _stack_maps(q_ref[0, :, hh * hd:(hh + 1) * hd]).T.astype(BF16)
    m_scr[...] = jnp.full(m_scr.shape, NEG_INF, F32)
    acc_scr[...] = jnp.zeros(acc_scr.shape, F32)
    far = [far_ref[h0 + hh] * LOG2E for hh in range(hp)]
    gw = min(blk, 2 * LANES)
    groups = [(hh, slice(g * gw, (g + 1) * gw)) for hh in range(hp) for g in range(2 * blk // gw)]

    def bias_of(hh, cs, nkeys):
        c0 = cs.start % blk
        if nkeys == 2 * blk:
            return bias_ref[hh, :, :, c0:c0 + gw].reshape(2 * blk, gw)
        return bias_ref[hh, 1, :, c0:c0 + gw]

    def tile(start, nkeys, biased):
        kj = [k_ref[0, pl.ds(start, nkeys), hh * hd:(hh + 1) * hd] for hh in range(hp)]
        vtj = [vt_ref[0, hh, :, pl.ds(start, nkeys)] for hh in range(hp)]
        s = [jnp.dot(kj[hh], qt_scr[hh, :, cs], preferred_element_type=F32) for hh, cs in groups]
        if biased:
            s = [sg + bias_of(hh, cs, nkeys) for sg, (hh, cs) in zip(s, groups)]
        p, alpha, m_new = [], [], []
        for sg, (hh, cs) in zip(s, groups):
            m_old = m_scr[hh, :, cs]
            m_cur = jnp.max(sg, axis=0, keepdims=True)
            if not biased:
                m_cur = m_cur + far[hh]
            mn = jnp.maximum(m_old, m_cur)
            alpha.append(jnp.exp2(m_old - mn))
            p.append(jnp.exp2(sg - (mn if biased else mn - far[hh])).astype(BF16))
            m_new.append(mn)
        pv = [jnp.dot(vtj[hh], pg, preferred_element_type=F32) for pg, (hh, cs) in zip(p, groups)]
        for x, (hh, cs) in enumerate(groups):
            acc_scr[hh, :, cs] = alpha[x] * acc_scr[hh, :, cs] + pv[x]
            m_scr[hh, :, cs] = m_new[x]

    n_far = jnp.maximum(i - 1, 0)

    def far_pair(jj, carry):
        tile(pl.multiple_of(jj * 2 * blk, 2 * blk), 2 * blk, False)
        return carry

    lax.fori_loop(0, n_far // 2, far_pair, 0)

    @pl.when(n_far % 2 == 1)
    def _():
        tile(pl.multiple_of((n_far - 1) * blk, blk), blk, False)

    @pl.when(i >= 1)
    def _():
        tile(pl.multiple_of((i - 1) * blk, blk), 2 * blk, True)

    @pl.when(i == 0)
    def _():
        tile(0, blk, True)

    lam = _lambda(lam_ref, lam_init)
    for hh in range(hp):
        acc = acc_scr[hh]
        wt = acc[:DV_B] / acc[DV_B:DV_B + 1]
        ot = wt[:, :blk] - lam * wt[:, blk:]
        ot = ot * lax.rsqrt(jnp.mean(ot * ot, axis=0, keepdims=True) + 1e-5)
        o_ref[0, :, hh * DV_B:(hh + 1) * DV_B] = (ot.T * sw_ref[...] * (1.0 - lam_init)).astype(o_ref.dtype)


def _attn_prompt(q, k16, vt16, rel_bias, lam_q, subln_w, lam_init, *, blk=512, hp=2):
    B, T, _ = q.shape
    blk = min(blk, T)
    bias = _prompt_bias_tiles(rel_bias, blk)
    far = rel_bias[NUM_BUCKETS - 1].astype(F32)
    vrows = vt16.shape[2]
    return pl.pallas_call(
        functools.partial(_attn_prompt_body, blk=blk, hp=hp, lam_init=lam_init),
        grid=(B, H_B // hp, T // blk),
        in_specs=[pl.BlockSpec(memory_space=pltpu.SMEM),
                  pl.BlockSpec((1, blk, hp * 2 * DK_B), lambda b, h, i: (b, i, h)),
                  pl.BlockSpec((1, T, hp * 2 * DK_B), lambda b, h, i: (b, 0, h)),
                  pl.BlockSpec((1, hp, vrows, T), lambda b, h, i: (b, h, 0, 0)),
                  pl.BlockSpec((hp, 2, blk, blk), lambda b, h, i: (h, 0, 0, 0)),
                  pl.BlockSpec((4, DK_B), lambda b, h, i: (0, 0)),
                  pl.BlockSpec((1, DV_B), lambda b, h, i: (0, 0))],
        out_specs=pl.BlockSpec((1, blk, hp * DV_B), lambda b, h, i: (b, i, h)),
        scratch_shapes=[pltpu.VMEM((hp, 2 * DK_B, 2 * blk), BF16), pltpu.VMEM((hp, 1, 2 * blk), F32),
                        pltpu.VMEM((hp, vrows, 2 * blk), F32)],
        out_shape=jax.ShapeDtypeStruct((B, T, H_B * DV_B), BF16),
        compiler_params=_params(("parallel", "parallel", "arbitrary")),
        name="attn_prompt",
    )(far, q, k16, vt16, bias, lam_q.astype(F32), subln_w.reshape(1, DV_B).astype(F32))


def _attn_sample_body(*refs, pp, t_new, lam_init):
    pt_ref, q_ref = refs[:2]
    k_refs = refs[2:2 + pp]
    v_refs = refs[2 + pp:2 + 2 * pp]
    kn_ref, vn_ref, bias_ref, lam_ref, sw_ref, o_ref, m_scr, l_scr, acc_scr = refs[2 + 2 * pp:]
    step = pl.program_id(1)
    last = step == pl.num_programs(1) - 1
    rows = 2 * t_new
    flat = PAGE_SIZE * H_B

    @pl.when(step == 0)
    def _():
        m_scr[...] = jnp.full(m_scr.shape, NEG_INF, F32)
        l_scr[...] = jnp.zeros(l_scr.shape, F32)
        acc_scr[...] = jnp.zeros(acc_scr.shape, F32)

    qall = jnp.concatenate([_stack_maps(q_ref[0, :, h * 2 * DK_B:(h + 1) * 2 * DK_B]) for h in range(H_B)],
                           axis=0).astype(BF16)

    def update(scores, values):
        m_old = m_scr[...]
        m_cur = jnp.max(scores[0], axis=1, keepdims=True)
        for s in scores[1:]:
            m_cur = jnp.maximum(m_cur, jnp.max(s, axis=1, keepdims=True))
        m_new = jnp.maximum(m_old, m_cur)
        alpha = jnp.exp2(m_old - m_new)
        l_new = alpha * l_scr[...]
        pv = None
        for s, v in zip(scores, values):
            p = jnp.exp2(s - m_new)
            l_new = l_new + jnp.sum(p, axis=1, keepdims=True)
            part = jnp.dot(p.astype(BF16), v, preferred_element_type=F32)
            pv = part if pv is None else pv + part
        l_scr[...] = l_new
        acc_scr[...] = alpha * acc_scr[...] + pv
        m_scr[...] = m_new

    def flat_page(ref):
        return ref[0].reshape(flat, 2 * DK_B).astype(BF16)

    scores = []
    for r in range(pp):
        bias = bias_ref[last.astype(jnp.int32)] if r == pp - 1 else bias_ref[0]
        scores.append(_dot_nt(qall, flat_page(k_refs[r])) + bias)
    update(scores, [flat_page(v_refs[r]) for r in range(pp)])

    @pl.when(last)
    def _():
        update([_dot_nt(qall, flat_page(kn_ref)) + bias_ref[2]], [flat_page(vn_ref)])
        w = acc_scr[...] / l_scr[...]
        lam = _lambda(lam_ref, lam_init)
        for h in range(H_B):
            o = w[h * rows:h * rows + t_new] - lam * w[h * rows + t_new:(h + 1) * rows]
            o = o * lax.rsqrt(jnp.mean(o * o, -1, keepdims=True) + 1e-5) * sw_ref[...] * (1.0 - lam_init)
            o_ref[0, :, h * DV_B:(h + 1) * DV_B] = o.astype(o_ref.dtype)


def _attn_sample(q, k_new, v_new, cache_k, cache_v, page_table, rel_bias, lam_q, subln_w, lam_init, *, pp=8):
    Bd, t_new, _ = q.shape
    n_pages = page_table.shape[1]
    assert n_pages % pp == 0
    bias = _sample_bias_tiles(rel_bias, t_new)
    padt = lambda a: jnp.pad(a, ((0, 0), (0, PAGE_SIZE - t_new), (0, 0), (0, 0)))
    page_spec = lambda r: pl.BlockSpec((1, PAGE_SIZE, H_B, 2 * DK_B),
                                       lambda b, s, pt, r=r: (pt[b, s * pp + r], 0, 0, 0))
    own_spec = pl.BlockSpec((1, PAGE_SIZE, H_B, 2 * DK_B), lambda b, s, pt: (b, 0, 0, 0))
    const = lambda *shape: pl.BlockSpec(shape, lambda b, s, pt: (0,) * len(shape))
    rows = H_B * 2 * t_new
    grid_spec = pltpu.PrefetchScalarGridSpec(
        num_scalar_prefetch=1,
        grid=(Bd, n_pages // pp),
        in_specs=([pl.BlockSpec((1, t_new, H_B * 2 * DK_B), lambda b, s, pt: (b, 0, 0))]
                  + [page_spec(r) for r in range(pp)] + [page_spec(r) for r in range(pp)]
                  + [own_spec, own_spec, const(3, rows, PAGE_SIZE * H_B), const(4, DK_B), const(1, DV_B)]),
        out_specs=pl.BlockSpec((1, t_new, H_B * DV_B), lambda b, s, pt: (b, 0, 0)),
        scratch_shapes=[pltpu.VMEM((rows, 1), F32), pltpu.VMEM((rows, 1), F32), pltpu.VMEM((rows, DV_B), F32)],
    )
    return pl.pallas_call(
        functools.partial(_attn_sample_body, pp=pp, t_new=t_new, lam_init=lam_init),
        grid_spec=grid_spec,
        out_shape=jax.ShapeDtypeStruct((Bd, t_new, H_B * DV_B), BF16),
        compiler_params=_params(("parallel", "arbitrary")),
        name="attn_sample",
    )(page_table, q, *([cache_k] * pp), *([cache_v] * pp), padt(k_new), padt(v_new), bias,
      lam_q.astype(F32), subln_w.reshape(1, DV_B).astype(F32))


def _gdn_mixer(x2d, B, T, w_qkv, w_zba, conv_w, a_log, dt_bias, gnorm_w, conv_state, ssm_state):
    state8 = jnp.pad(conv_state.astype(F32), ((0, 0), (8 - (CONV_W - 1), 0), (0, 0)))
    fused_tile = 512
    if T % fused_tile == 0:
        qkvc, tail8 = _inproj_conv(x2d, w_qkv, state8, conv_w.astype(F32), T, tm=fused_tile)
        qkvc = qkvc.reshape(B, T, CONV_DIM)
        xpad_tail = tail8[:, 8 - (CONV_W - 1):]
    else:
        proj = _mm(x2d, w_qkv).reshape(B, T, CONV_DIM)
        qkvc = _gdn_prep(proj, state8, conv_w.astype(F32), tt=fused_tile)
        xpad_tail = jnp.concatenate([conv_state.astype(F32), proj], axis=1)[:, -(CONV_W - 1):]
    zba2d = _mm(x2d, w_zba, tm=512)
    proj = zba2d.reshape(B, T, -1)
    bgc = _gdn_gates(zba2d, VDIM_A // LANES, a_log, dt_bias, min(CHUNK, T)).reshape(B, T, LANES)
    if T % CHUNK:
        tp = -(-T // CHUNK) * CHUNK
        padt = lambda a: jnp.pad(a, ((0, 0), (0, tp - T), (0, 0)))
        lane = jnp.arange(LANES)
        is_gc = (lane >= 2 * H_V) & (lane < 3 * H_V)
        tail = jnp.broadcast_to(jnp.where(is_gc, bgc[:, T - 1:T], 0.0), (B, tp - T, LANES))
        bgc = jnp.concatenate([bgc, tail], axis=1)
        qkvc = padt(qkvc)
        z_arr = padt(proj[:, :, :VDIM_A])
    else:
        z_arr = proj
    u, w, qg, kd, aqk, dec = _gdn_intra(qkvc, bgc)
    y, s_new = _gdn_state(u, w, qg, kd, aqk, dec, z_arr, 0, ssm_state, gnorm_w)
    return y[:, :T].reshape(B * T, VDIM_A), xpad_tail, s_new


def _mlp_and_embed(x, p2d, ln_g, ln_b, w_up, w_down, w_ple, w_pg):
    return _mlp_gate(x, p2d, w_up, w_down, ln_g, ln_b, w_ple, w_pg)


def kernel(x_prompt, x_sample, state_conv, state_ssm, cache_k, cache_v, page_table, p_prompt, p_sample,
           w_in_a, conv_w_a, a_log_a, dt_bias_a, gnorm_w_a, w_out_a, w_kv, w_q_b, lam_b, subln_w_b, w_o_b,
           rel_bias, ln_g, ln_b, w_up, w_down, w_ple, w_pg):
    bf = lambda w: w.astype(BF16)
    w_qkv = bf(w_in_a[:, :, :CONV_DIM])
    w_zba = bf(jnp.pad(w_in_a[:, :, CONV_DIM:], ((0, 0), (0, 0), (0, IN_A_PAD - IN_A))))
    w_out, w_kv16, w_q, w_o = bf(w_out_a), bf(w_kv), bf(w_q_b), bf(w_o_b)
    w_up16, w_down16, w_ple16, w_pg16 = bf(w_up), bf(w_down), bf(w_ple), bf(w_pg)

    def run(x3d, p, conv_state, ssm_state, attn_fn):
        B, T, _ = x3d.shape
        x = x3d.reshape(B * T, D_MODEL)
        convs, ssms = [], []
        kv = kv16 = None
        for i in range(DEPTH):
            if i < N_A:
                y, cs, ss = _gdn_mixer(x, B, T, w_qkv[i], w_zba[i], conv_w_a[i], a_log_a[i], dt_bias_a[i],
                                       gnorm_w_a[i], conv_state[i], ssm_state[i])
                convs.append(cs)
                ssms.append(ss)
                x = _mm_ln(y, w_out[i], x, ln_g[i, 0], ln_b[i, 0])
            else:
                j = i - N_A
                lam_init = 0.8 - 0.6 * math.exp(-0.3 * i)
                q = _mm(x, w_q[j]).reshape(B, T, H_B * 2 * DK_B)
                o = attn_fn(q, kv, kv16, lam_b[j], subln_w_b[j], lam_init)
                x = _mm_ln(o.reshape(B * T, H_B * DV_B), w_o[j], x, ln_g[i, 0], ln_b[i, 0])
            x = _mlp_and_embed(x, p[i].reshape(B * T, -1), ln_g[i, 1], ln_b[i, 1], w_up16[i], w_down16[i],
                               w_ple16[i], w_pg16[i])
            if i == N_A - 1:
                kv, kv16 = _mm(x, w_kv16, out_dtypes=(F32, BF16), tn=1024)
                kv, kv16 = kv.reshape(B, T, -1), kv16.reshape(B, T, -1)
        k_sh = kv[..., :H_B * 2 * DK_B].reshape(B, T, H_B, 2 * DK_B)
        v_sh = kv[..., H_B * 2 * DK_B:].reshape(B, T, H_B, DV_B)
        return x.reshape(B, T, D_MODEL), jnp.stack(convs), jnp.stack(ssms), k_sh, v_sh

    def attn_prompt(q, kv, kv16, lam_q, sw, lam_init):
        B, T, _ = q.shape
        vt16 = kv16[..., H_B * 2 * DK_B:].reshape(B, T, H_B, DV_B).transpose(0, 2, 3, 1)
        ones = jnp.ones((B, H_B, 16, T), BF16)
        return _attn_prompt(q, kv16, jnp.concatenate([vt16, ones], axis=2), rel_bias, lam_q, sw, lam_init)

    def attn_sample(q, kv, kv16, lam_q, sw, lam_init):
        B, T, _ = q.shape
        k_new = kv[..., :H_B * 2 * DK_B].reshape(B, T, H_B, 2 * DK_B)
        v_new = kv[..., H_B * 2 * DK_B:].reshape(B, T, H_B, DV_B)
        return _attn_sample(q, k_new, v_new, cache_k, cache_v, page_table, rel_bias, lam_q, sw, lam_init)

    Bp = x_prompt.shape[0]
    conv0 = jnp.zeros((N_A, Bp, CONV_W - 1, CONV_DIM), x_prompt.dtype)
    ssm0 = jnp.zeros((N_A, Bp, H_V, DK_A, DV_A), state_ssm.dtype)
    y_prompt, conv_p, ssm_p, k_p, v_p = run(x_prompt, p_prompt, conv0, ssm0, attn_prompt)
    y_sample, conv_s, ssm_s, k_s, v_s = run(x_sample, p_sample, state_conv, state_ssm, attn_sample)
    return (y_prompt, y_sample, conv_p, ssm_p, k_p, v_p, conv_s, ssm_s, k_s, v_s)
```

```python
import functools
import math

import jax
import jax.numpy as jnp
from jax import lax
from jax.experimental import pallas as pl
from jax.experimental.pallas import tpu as pltpu

F32 = jnp.float32
BF16 = jnp.bfloat16

D_MODEL = 1024
DEPTH = 2
N_A = DEPTH // 2
PAGE_SIZE = 128
H_QK = 8
H_V = 16
DK_A = 128
DV_A = 128
CONV_W = 4
CHUNK = 64
QK_DIM = H_QK * DK_A
VDIM_A = H_V * DV_A
CONV_DIM = 2 * QK_DIM + VDIM_A
IN_A = CONV_DIM + VDIM_A + 2 * H_V
IN_A_PAD = CONV_DIM + VDIM_A + 128
H_B = 8
DK_B = 64
DV_B = 128
NUM_BUCKETS = 32
MAX_DISTANCE = 128
D_FF = 4 * D_MODEL
ALPHA = (2 * DEPTH) ** 0.25

LANES = 128
VMEM_LIMIT = 56 * 1024 * 1024
NEG_INF = float("-inf")
LOG2E = math.log2(math.e)


def _params(sem, vmem=VMEM_LIMIT):
    return pltpu.CompilerParams(dimension_semantics=sem, vmem_limit_bytes=vmem)


def _sigmoid(x):
    return 1.0 / (1.0 + jnp.exp(-x))


def _dot_nt(a, b):
    return lax.dot_general(a, b, (((1,), (1,)), ((), ())), preferred_element_type=F32)


def _dot_tn(a, b):
    return lax.dot_general(a, b, (((0,), (0,)), ((), ())), preferred_element_type=F32)


def _mm_body(x_ref, w_ref, *o_refs, act):
    acc = jnp.dot(x_ref[...].astype(BF16), w_ref[...], preferred_element_type=F32)
    if act == "relu2":
        r = jnp.maximum(acc, 0.0)
        acc = r * r
    for o_ref in o_refs:
        o_ref[...] = acc.astype(o_ref.dtype)


def _mm(x, w, *, act=None, out_dtypes=(F32,), tm=1024, tn=None):
    M, K = x.shape
    N = w.shape[1]
    tm = min(tm, M)
    tn = N if tn is None else tn
    assert M % tm == 0 and N % tn == 0
    outs = pl.pallas_call(
        functools.partial(_mm_body, act=act),
        grid=(M // tm, N // tn),
        in_specs=[pl.BlockSpec((tm, K), lambda i, j: (i, 0)),
                  pl.BlockSpec((K, tn), lambda i, j: (0, j))],
        out_specs=[pl.BlockSpec((tm, tn), lambda i, j: (i, j)) for _ in out_dtypes],
        out_shape=[jax.ShapeDtypeStruct((M, N), dt) for dt in out_dtypes],
        compiler_params=_params(("parallel", "arbitrary")),
        name="mm",
    )(x, w)
    return outs[0] if len(out_dtypes) == 1 else outs


V_ONES_ROWS = 16


def _kv_proj_body(x_ref, w_ref, k_ref, v_ref, k16_ref, vt16_ref):
    acc = jnp.dot(x_ref[...].astype(BF16), w_ref[...], preferred_element_type=F32)
    nk = H_B * 2 * DK_B
    k_ref[...] = pltpu.einshape("m(hd)->mhd", acc[:, :nk], h=H_B)
    v_ref[...] = pltpu.einshape("m(hd)->mhd", acc[:, nk:], h=H_B)
    k16_ref[...] = acc[:, :nk].astype(BF16)
    for h in range(H_B):
        vt16_ref[0, h, :DV_B, :] = acc[:, nk + h * DV_B:nk + (h + 1) * DV_B].T.astype(BF16)
        vt16_ref[0, h, DV_B:, :] = jnp.ones((V_ONES_ROWS, acc.shape[0]), BF16)


def _kv_proj(x, w_kv, T, *, tm=512):
    M, K = x.shape
    N = w_kv.shape[1]
    tm = min(tm, T)
    assert T % tm == 0
    tps = T // tm
    nk = H_B * 2 * DK_B
    return pl.pallas_call(
        _kv_proj_body,
        grid=(M // tm,),
        in_specs=[pl.BlockSpec((tm, K), lambda i: (i, 0)),
                  pl.BlockSpec((K, N), lambda i: (0, 0), pipeline_mode=pl.Buffered(1))],
        out_specs=[pl.BlockSpec((tm, H_B, 2 * DK_B), lambda i: (i, 0, 0)),
                   pl.BlockSpec((tm, H_B, DV_B), lambda i: (i, 0, 0)),
                   pl.BlockSpec((tm, nk), lambda i: (i, 0)),
                   pl.BlockSpec((1, H_B, DV_B + V_ONES_ROWS, tm), lambda i: (i // tps, 0, 0, i % tps))],
        out_shape=[jax.ShapeDtypeStruct((M, H_B, 2 * DK_B), F32), jax.ShapeDtypeStruct((M, H_B, DV_B), F32),
                   jax.ShapeDtypeStruct((M, nk), BF16),
                   jax.ShapeDtypeStruct((M // T, H_B, DV_B + V_ONES_ROWS, T), BF16)],
        compiler_params=_params(("parallel",)),
        name="kv_proj",
    )(x, w_kv)


def _mm_ln_body(x_ref, w_ref, r_ref, g_ref, b_ref, o_ref):
    acc = jnp.dot(x_ref[...].astype(BF16), w_ref[...], preferred_element_type=F32)
    y = ALPHA * r_ref[...] + acc
    mu = jnp.mean(y, -1, keepdims=True)
    yc = y - mu
    var = jnp.mean(yc * yc, -1, keepdims=True)
    o_ref[...] = yc * lax.rsqrt(var + 1e-5) * g_ref[...] + b_ref[...]


def _mm_ln(x, w, res, g, b, *, tm=512):
    M, K = x.shape
    N = w.shape[1]
    tm = min(tm, M)
    return pl.pallas_call(
        _mm_ln_body,
        grid=(M // tm,),
        in_specs=[pl.BlockSpec((tm, K), lambda i: (i, 0)),
                  pl.BlockSpec((K, N), lambda i: (0, 0)),
                  pl.BlockSpec((tm, N), lambda i: (i, 0)),
                  pl.BlockSpec((1, N), lambda i: (0, 0)),
                  pl.BlockSpec((1, N), lambda i: (0, 0))],
        out_specs=pl.BlockSpec((tm, N), lambda i: (i, 0)),
        out_shape=jax.ShapeDtypeStruct((M, N), F32),
        compiler_params=_params(("parallel",)),
        name="mm_ln",
    )(x, w, res, g.reshape(1, N), b.reshape(1, N))


def _layernorm_rows(y, g, b):
    mu = jnp.mean(y, -1, keepdims=True)
    yc = y - mu
    var = jnp.mean(yc * yc, -1, keepdims=True)
    return yc * lax.rsqrt(var + 1e-5) * g + b


def _mlp_gate_body(x_ref, p_ref, wup_ref, wdown_ref, g_ref, b_ref, wple_ref, wpg_ref, o_ref):
    x = x_ref[...]
    u = jnp.maximum(jnp.dot(x.astype(BF16), wup_ref[...], preferred_element_type=F32), 0.0)
    h = (u * u).astype(BF16)
    x2 = _layernorm_rows(ALPHA * x + jnp.dot(h, wdown_ref[...], preferred_element_type=F32), g_ref[...], b_ref[...])
    e = jnp.dot(p_ref[...].astype(BF16), wple_ref[...], preferred_element_type=F32)
    gt = jnp.dot(x2.astype(BF16), wpg_ref[...], preferred_element_type=F32)
    o_ref[...] = x2 + e * _sigmoid(gt)


def _mlp_gate(x, p, w_up, w_down, g, b, w_ple, w_pg, *, tm=512):
    M, D = x.shape
    P = p.shape[1]
    F = w_up.shape[1]
    tm = min(tm, M)
    resident = lambda r, c: pl.BlockSpec((r, c), lambda i: (0, 0), pipeline_mode=pl.Buffered(1))
    return pl.pallas_call(
        _mlp_gate_body,
        grid=(M // tm,),
        in_specs=[pl.BlockSpec((tm, D), lambda i: (i, 0)),
                  pl.BlockSpec((tm, P), lambda i: (i, 0)),
                  resident(D, F), resident(F, D), resident(1, D), resident(1, D), resident(P, D), resident(D, D)],
        out_specs=pl.BlockSpec((tm, D), lambda i: (i, 0)),
        out_shape=jax.ShapeDtypeStruct((M, D), F32),
        compiler_params=_params(("parallel",)),
        name="mlp_gate",
    )(x, p, w_up, w_down, g.reshape(1, D), b.reshape(1, D), w_ple, w_pg)


def _conv_silu_norm(xf_ref, cw_ref, store, first_group):
    tt = xf_ref.shape[0] - 8
    for h in range(xf_ref.shape[1] // LANES):
        sl = slice(h * LANES, (h + 1) * LANES)
        cw = cw_ref[:, sl]
        c = xf_ref[8:, sl] * cw[3:4]
        for s in (1, 2, 3):
            c = c + xf_ref[8 - s:8 - s + tt, sl] * cw[3 - s:4 - s]
        c = c * _sigmoid(c)
        group = first_group + h
        nrm = lax.rsqrt(jnp.sum(c * c, -1, keepdims=True) + 1e-6)
        scale = jnp.where(group < H_QK, nrm * (DK_A ** -0.5), jnp.where(group < 2 * H_QK, nrm, 1.0))
        store(sl, c * scale)


def _gdn_prep_body(x_ref, halo_ref, st_ref, cw_ref, o_ref, xf_scr, *, gw, nb):
    t = pl.program_id(1)
    j = pl.program_id(2)
    for bi in range(nb):
        xf_scr[:8] = jnp.where(t == 0, st_ref[bi], halo_ref[bi])
        xf_scr[8:] = x_ref[bi]

        def store(sl, val, bi=bi):
            o_ref[bi, :, sl] = val

        _conv_silu_norm(xf_scr, cw_ref, store, j * (gw // LANES))


def _inproj_conv_body(x_ref, w_ref, st_ref, cw_ref, o_ref, tail_ref, carry_scr, xf_scr, *, tiles_per_seq):
    i = pl.program_id(0)
    j = pl.program_id(1)
    tn = w_ref.shape[1]
    xf_scr[:8] = jnp.where(i % tiles_per_seq == 0, st_ref[0], carry_scr[j])
    xf_scr[8:] = jnp.dot(x_ref[...].astype(BF16), w_ref[...], preferred_element_type=F32)

    def store(sl, val):
        o_ref[:, sl] = val

    _conv_silu_norm(xf_scr, cw_ref, store, j * (tn // LANES))
    last8 = xf_scr[xf_scr.shape[0] - 8:]
    carry_scr[j] = last8
    tail_ref[0] = last8


def _inproj_conv(x2d, w_qkv, state8, conv_w, T, *, tm=512, tn=1024):
    M, K = x2d.shape
    assert T % tm == 0 and CONV_DIM % tn == 0
    tps = T // tm
    qkvc, tails = pl.pallas_call(
        functools.partial(_inproj_conv_body, tiles_per_seq=tps),
        grid=(M // tm, CONV_DIM // tn),
        in_specs=[pl.BlockSpec((tm, K), lambda i, j: (i, 0)),
                  pl.BlockSpec((K, tn), lambda i, j: (0, j)),
                  pl.BlockSpec((1, 8, tn), lambda i, j: (i // tps, 0, j)),
                  pl.BlockSpec((CONV_W, tn), lambda i, j: (0, j))],
        out_specs=[pl.BlockSpec((tm, tn), lambda i, j: (i, j)),
                   pl.BlockSpec((1, 8, tn), lambda i, j: (i, 0, j))],
        out_shape=[jax.ShapeDtypeStruct((M, CONV_DIM), F32), jax.ShapeDtypeStruct((M // tm, 8, CONV_DIM), F32)],
        scratch_shapes=[pltpu.VMEM((CONV_DIM // tn, 8, tn), F32), pltpu.VMEM((tm + 8, tn), F32)],
        compiler_params=_params(("arbitrary", "arbitrary")),
        name="inproj_conv",
    )(x2d, w_qkv, state8, conv_w)
    return qkvc, tails[tps - 1::tps]


def _gdn_prep(proj, state8, conv_w, *, tt, gw=512):
    B, T, _ = proj.shape
    tt = min(tt, T)
    hb = tt // 8
    nb = min(max(64 // tt, 1), B)
    assert B % nb == 0
    return pl.pallas_call(
        functools.partial(_gdn_prep_body, gw=gw, nb=nb),
        grid=(B // nb, T // tt, CONV_DIM // gw),
        in_specs=[pl.BlockSpec((nb, tt, gw), lambda b, t, j: (b, t, j)),
                  pl.BlockSpec((nb, 8, gw), lambda b, t, j: (b, jnp.maximum(t * hb - 1, 0), j)),
                  pl.BlockSpec((nb, 8, gw), lambda b, t, j: (b, 0, j)),
                  pl.BlockSpec((CONV_W, gw), lambda b, t, j: (0, j))],
        out_specs=pl.BlockSpec((nb, tt, gw), lambda b, t, j: (b, t, j)),
        out_shape=jax.ShapeDtypeStruct((B, T, CONV_DIM), F32),
        scratch_shapes=[pltpu.VMEM((tt + 8, gw), F32)],
        compiler_params=_params(("parallel", "parallel", "parallel")),
        name="gdn_prep",
    )(proj, proj, state8, conv_w)


def _gdn_gates_body(x_ref, alog_ref, dtb_ref, o_ref, *, period):
    ba = x_ref[...]
    lane = lax.broadcasted_iota(jnp.int32, ba.shape, 1)
    beta = _sigmoid(ba)
    xx = ba + dtb_ref[...]
    softplus = jnp.maximum(xx, 0.0) + jnp.log(1.0 + jnp.exp(-jnp.abs(xx)))
    g = -jnp.exp(alog_ref[...]) * softplus
    pos = lax.broadcasted_iota(jnp.int32, ba.shape, 0) % period
    gc = g
    s = 1
    while s < period:
        gc = gc + jnp.where(pos >= s, pltpu.roll(gc, s, 0), 0.0)
        s *= 2
    gc = pltpu.roll(gc, H_V, 1)
    o_ref[...] = jnp.where(lane < H_V, beta,
                           jnp.where(lane < 2 * H_V, g, jnp.where(lane < 3 * H_V, gc, 0.0)))


def _gdn_gates(proj2d, col, a_log, dt_bias, period, *, tm=1024):
    M = proj2d.shape[0]
    tm = min(tm, M)
    assert tm % period == 0
    pad = lambda v: jnp.pad(v.astype(F32), (H_V, LANES - 2 * H_V)).reshape(1, LANES)
    return pl.pallas_call(
        functools.partial(_gdn_gates_body, period=period),
        grid=(M // tm,),
        in_specs=[pl.BlockSpec((tm, LANES), lambda i: (i, col)),
                  pl.BlockSpec((1, LANES), lambda i: (0, 0)),
                  pl.BlockSpec((1, LANES), lambda i: (0, 0))],
        out_specs=pl.BlockSpec((tm, LANES), lambda i: (i, 0)),
        out_shape=jax.ShapeDtypeStruct((M, LANES), F32),
        compiler_params=_params(("parallel",)),
        name="gdn_gates",
    )(proj2d, pad(a_log), pad(dt_bias))


def _gdn_intra_body(q_ref, k_ref, v_ref, bgc_ref, gct_ref, u_ref, w_ref, qg_ref, kd_ref, aqk_ref, dec_ref,
                    *, nb, ca):
    hk = pl.program_id(2)
    C = CHUNK
    row = lax.broadcasted_iota(jnp.int32, (C, C), 0)
    col = lax.broadcasted_iota(jnp.int32, (C, C), 1)
    incl = row >= col
    strict = row > col
    eye = (row == col).astype(F32)
    lane = lax.broadcasted_iota(jnp.int32, (C, LANES), 1)
    blocks = [(bi, ci) for bi in range(nb) for ci in range(ca)]
    chains = [(x, e) for x in range(len(blocks)) for e in range(2)]
    at = lambda x: (blocks[x][0], slice(blocks[x][1] * C, (blocks[x][1] + 1) * C))
    full = (slice(None),)
    k16 = [k_ref[at(x) + full].astype(BF16) for x in range(len(blocks))]
    kk = [_dot_nt(a, a) for a in k16]
    qk = [_dot_nt(q_ref[at(x) + full].astype(BF16), k16[x]) for x in range(len(blocks))]
    beta, gc, npow = [], [], []
    for x, e in chains:
        bi, ci = blocks[x]
        hv = 2 * hk + e
        bgc = bgc_ref[at(x) + full]
        b = jnp.sum(jnp.where(lane == hv, bgc, 0.0), axis=1, keepdims=True)
        g = jnp.sum(jnp.where(lane == 2 * H_V + hv, bgc, 0.0), axis=1, keepdims=True)
        dm = jnp.exp(jnp.where(incl, g - gct_ref[bi, ci, pl.ds(hv, 1), :], NEG_INF))
        aqk_ref[at(x) + (slice(e * C, (e + 1) * C),)] = jnp.where(incl, qk[x] * dm, 0.0).astype(BF16)
        npow.append(jnp.where(strict, -(b * kk[x] * dm), 0.0))
        beta.append(b)
        gc.append(g)
    tinv = [eye + n for n in npow]
    for _ in range(5):
        n16 = [n.astype(BF16) for n in npow]
        npow = [jnp.dot(a, a, preferred_element_type=F32) for a in n16]
        tinv = [t + jnp.dot(t.astype(BF16), n.astype(BF16), preferred_element_type=F32)
                for t, n in zip(tinv, npow)]
    eg = [jnp.exp(g) for g in gc]
    rhs = []
    for y, (x, e) in enumerate(chains):
        k = k_ref[at(x) + full]
        v = v_ref[at(x) + (slice(e * DV_A, (e + 1) * DV_A),)]
        rhs.append(jnp.concatenate([v * beta[y], k * (beta[y] * eg[y])], axis=1).astype(BF16))
    uw = [jnp.dot(t.astype(BF16), r, preferred_element_type=F32) for t, r in zip(tinv, rhs)]
    for y, (x, e) in enumerate(chains):
        bi, ci = blocks[x]
        hs = at(x) + (slice(e * DV_A, (e + 1) * DV_A),)
        u_ref[hs] = uw[y][:, :DV_A].astype(BF16)
        w_ref[hs] = uw[y][:, DV_A:].astype(BF16)
        qg_ref[hs] = (q_ref[at(x) + full] * eg[y]).astype(BF16)
        g_last = gc[y][C - 1:C, :]
        kd_ref[hs] = (k_ref[at(x) + full] * jnp.exp(g_last - gc[y])).astype(BF16)
        dec_ref[bi, 0, ci, e:e + 1, :] = jnp.broadcast_to(jnp.exp(g_last), (1, LANES))


def _gdn_intra(qkvc, bgc, *, chains=32):
    B, T, _ = qkvc.shape
    nc = T // CHUNK
    ca = min(chains // 2, nc)
    nb = min(max(chains // (2 * ca), 1), B)
    assert nc % ca == 0 and B % nb == 0
    ta = ca * CHUNK
    gct = bgc[..., 2 * H_V:3 * H_V].reshape(B, nc, CHUNK, H_V).transpose(0, 1, 3, 2)
    kb = QK_DIM // DK_A
    wide = lambda: pl.BlockSpec((nb, ta, 2 * DV_A), lambda b, t, h: (b, t, h))
    return pl.pallas_call(
        functools.partial(_gdn_intra_body, nb=nb, ca=ca),
        grid=(B // nb, T // ta, H_QK),
        in_specs=[pl.BlockSpec((nb, ta, DK_A), lambda b, t, h: (b, t, h)),
                  pl.BlockSpec((nb, ta, DK_A), lambda b, t, h: (b, t, kb + h)),
                  pl.BlockSpec((nb, ta, 2 * DV_A), lambda b, t, h: (b, t, kb + h)),
                  pl.BlockSpec((nb, ta, LANES), lambda b, t, h: (b, t, 0)),
                  pl.BlockSpec((nb, ca, H_V, CHUNK), lambda b, t, h: (b, t, 0, 0))],
        out_specs=[wide(), wide(), wide(), wide(),
                   pl.BlockSpec((nb, ta, 2 * CHUNK), lambda b, t, h: (b, t, h)),
                   pl.BlockSpec((nb, 1, ca, 2, LANES), lambda b, t, h: (b, h, t, 0, 0))],
        out_shape=[jax.ShapeDtypeStruct((B, T, VDIM_A), BF16)] * 4
        + [jax.ShapeDtypeStruct((B, T, H_V * CHUNK), BF16),
           jax.ShapeDtypeStruct((B, H_QK, nc, 2, LANES), F32)],
        compiler_params=_params(("parallel", "parallel", "arbitrary")),
        name="gdn_intra",
    )(qkvc, qkvc, qkvc, bgc, gct)


def _gdn_state_body(u_ref, w_ref, qg_ref, kd_ref, aqk_ref, dec_ref, z_ref, s0_ref, gw_ref, y_ref, st_ref, s_scr,
                    *, cb, hg):
    t = pl.program_id(2)
    C = CHUNK

    @pl.when(t == 0)
    def _():
        s_scr[...] = s0_ref[0]

    def chunk(ci, carry):
        rows = pl.ds(pl.multiple_of(ci * C, C), C)
        cols = [slice(hh * DV_A, (hh + 1) * DV_A) for hh in range(hg)]
        s = [s_scr[hh] for hh in range(hg)]
        r = [jnp.dot(jnp.concatenate([w_ref[0, rows, cs], qg_ref[0, rows, cs]], axis=0), s[hh].astype(BF16),
                     preferred_element_type=F32) for hh, cs in enumerate(cols)]
        v_new = [(u_ref[0, rows, cs].astype(F32) - r[hh][:C]).astype(BF16) for hh, cs in enumerate(cols)]
        s_upd = [_dot_tn(kd_ref[0, rows, cs], v_new[hh]) for hh, cs in enumerate(cols)]
        o_in = [jnp.dot(aqk_ref[0, rows, hh * C:(hh + 1) * C], v_new[hh], preferred_element_type=F32)
                for hh in range(hg)]
        for hh, cs in enumerate(cols):
            dec = dec_ref[0, hh // 2, pl.ds(ci, 1), hh % 2, :]
            s_scr[hh] = s[hh] * dec + s_upd[hh]
            o = r[hh][C:] + o_in[hh]
            z = z_ref[0, rows, cs]
            y = o * lax.rsqrt(jnp.mean(o * o, -1, keepdims=True) + 1e-6) * gw_ref[...] * (z * _sigmoid(z))
            y_ref[0, rows, cs] = y.astype(y_ref.dtype)
        return carry

    lax.fori_loop(0, cb, chunk, 0)

    @pl.when(t == pl.num_programs(2) - 1)
    def _():
        st_ref[0] = s_scr[...]


def _gdn_state(u, w, qg, kd, aqk, dec, z_arr, z_off, s0, gnorm_w, *, cb=8, hg=16):
    B, T, _ = u.shape
    nc = T // CHUNK
    cb = min(cb, nc)
    tb = cb * CHUNK
    ng = H_V // hg
    assert z_off % (hg * DV_A) == 0
    z_col = z_off // (hg * DV_A)
    wide = lambda: pl.BlockSpec((1, tb, hg * DV_A), lambda b, g, t: (b, t, g))
    return pl.pallas_call(
        functools.partial(_gdn_state_body, cb=cb, hg=hg),
        grid=(B, ng, T // tb),
        in_specs=[wide(), wide(), wide(), wide(),
                  pl.BlockSpec((1, tb, hg * CHUNK), lambda b, g, t: (b, t, g)),
                  pl.BlockSpec((1, hg // 2, cb, 2, LANES), lambda b, g, t: (b, g, t, 0, 0)),
                  pl.BlockSpec((1, tb, hg * DV_A), lambda b, g, t: (b, t, z_col + g)),
                  pl.BlockSpec((1, hg, DK_A, DV_A), lambda b, g, t: (b, g, 0, 0)),
                  pl.BlockSpec((1, DV_A), lambda b, g, t: (0, 0))],
        out_specs=[wide(), pl.BlockSpec((1, hg, DK_A, DV_A), lambda b, g, t: (b, g, 0, 0))],
        out_shape=[jax.ShapeDtypeStruct((B, T, VDIM_A), BF16),
                   jax.ShapeDtypeStruct((B, H_V, DK_A, DV_A), F32)],
        scratch_shapes=[pltpu.VMEM((hg, DK_A, DV_A), F32)],
        compiler_params=_params(("parallel", "parallel", "arbitrary")),
        name="gdn_state",
    )(u, w, qg, kd, aqk, dec, z_arr, s0, gnorm_w.reshape(1, DV_A).astype(F32))


def _bucket_starts():
    max_exact = NUM_BUCKETS // 2
    starts = list(range(max_exact + 1))
    n = max_exact
    for b in range(max_exact + 1, NUM_BUCKETS):
        while max_exact + int(math.log(n / max_exact) / math.log(MAX_DISTANCE / max_exact)
                              * (NUM_BUCKETS - max_exact)) < b:
            n += 1
        starts.append(n)
    return starts


_BUCKET_START = _bucket_starts()


def _bias_of_distance(n, tab_ref, h):
    val = jnp.full(n.shape, tab_ref[(NUM_BUCKETS - 1) * H_B + h], F32)
    for b in range(NUM_BUCKETS - 2, -1, -1):
        val = jnp.where(n < _BUCKET_START[b + 1], tab_ref[b * H_B + h], val)
    return val


def _prompt_bias_body(tab_ref, o_ref, *, blk):
    h = pl.program_id(0)
    key = lax.broadcasted_iota(jnp.int32, (blk, blk), 0)
    qry = lax.broadcasted_iota(jnp.int32, (blk, blk), 1)
    d = qry - key
    o_ref[0, 0] = LOG2E * _bias_of_distance(d + blk, tab_ref, h)
    o_ref[0, 1] = jnp.where(d >= 0, LOG2E * _bias_of_distance(jnp.maximum(d, 0), tab_ref, h), NEG_INF)


def _prompt_bias_tiles(rel_bias, blk):
    assert blk >= _BUCKET_START[NUM_BUCKETS - 1]
    return pl.pallas_call(
        functools.partial(_prompt_bias_body, blk=blk),
        grid=(H_B,),
        in_specs=[pl.BlockSpec(memory_space=pltpu.SMEM)],
        out_specs=pl.BlockSpec((1, 2, blk, blk), lambda h: (h, 0, 0, 0)),
        out_shape=jax.ShapeDtypeStruct((H_B, 2, blk, blk), F32),
        compiler_params=_params(("arbitrary",)),
        name="prompt_bias",
    )(rel_bias.astype(F32).reshape(-1))


def _sample_bias_body(tab_ref, o_ref, *, t_new):
    h = pl.program_id(0)
    shape = (2 * t_new, PAGE_SIZE * H_B)
    r = lax.broadcasted_iota(jnp.int32, shape, 0)
    j = lax.broadcasted_iota(jnp.int32, shape, 1)
    tok = jnp.where(r >= t_new, r - t_new, r)
    c = j // H_B
    same_head = (j % H_B) == h
    far = jnp.full(shape, LOG2E * tab_ref[(NUM_BUCKETS - 1) * H_B + h], F32)
    o_ref[0] = jnp.where(same_head, far, NEG_INF)
    o_ref[1] = jnp.where(same_head, LOG2E * _bias_of_distance(PAGE_SIZE + tok - c, tab_ref, h), NEG_INF)
    d = tok - c
    o_ref[2] = jnp.where(same_head & (d >= 0), LOG2E * _bias_of_distance(jnp.maximum(d, 0), tab_ref, h), NEG_INF)


def _sample_bias_tiles(rel_bias, t_new):
    rows = 2 * t_new
    return pl.pallas_call(
        functools.partial(_sample_bias_body, t_new=t_new),
        grid=(H_B,),
        in_specs=[pl.BlockSpec(memory_space=pltpu.SMEM)],
        out_specs=pl.BlockSpec((3, rows, PAGE_SIZE * H_B), lambda h: (0, h, 0)),
        out_shape=jax.ShapeDtypeStruct((3, H_B * rows, PAGE_SIZE * H_B), F32),
        compiler_params=_params(("arbitrary",)),
        name="sample_bias",
    )(rel_bias.astype(F32).reshape(-1))


def _lambda(lam_ref, lam_init):
    lq = lam_ref[...]
    a = jnp.sum(lq[0:1] * lq[1:2], axis=1, keepdims=True)
    b = jnp.sum(lq[2:3] * lq[3:4], axis=1, keepdims=True)
    return jnp.exp(a) - jnp.exp(b) + lam_init


def _stack_maps(q):
    lane = lax.broadcasted_iota(jnp.int32, q.shape, 1)
    q = q * (DK_B ** -0.5 * LOG2E)
    return jnp.concatenate([jnp.where(lane < DK_B, q, 0.0), jnp.where(lane >= DK_B, q, 0.0)], axis=0)


def _attn_prompt_body(far_ref, q_ref, k_ref, vt_ref, bias_ref, lam_ref, sw_ref, o_ref,
                      qt_scr, m_scr, acc_scr, *, blk, hp, lam_init):
    h0 = pl.program_id(1) * hp
    i = pl.program_id(2)
    hd = 2 * DK_B
    for hh in range(hp):
        qt_scr[hh] = _stack_maps(q_ref[0, :, hh * hd:(hh + 1) * hd]).T.astype(BF16)
    m_scr[...] = jnp.full(m_scr.shape, NEG_INF, F32)
    acc_scr[...] = jnp.zeros(acc_scr.shape, F32)
    far = [far_ref[h0 + hh] * LOG2E for hh in range(hp)]
    gw = min(blk, 2 * LANES)
    groups = [(hh, slice(g * gw, (g + 1) * gw)) for hh in range(hp) for g in range(2 * blk // gw)]

    def bias_of(hh, cs, nkeys):
        c0 = cs.start % blk
        if nkeys == 2 * blk:
            return bias_ref[hh, :, :, c0:c0 + gw].reshape(2 * blk, gw)
        return bias_ref[hh, 1, :, c0:c0 + gw]

    def tile(start, nkeys, biased):
        kj = [k_ref[0, pl.ds(start, nkeys), hh * hd:(hh + 1) * hd] for hh in range(hp)]
        vtj = [vt_ref[0, hh, :, pl.ds(start, nkeys)] for hh in range(hp)]
        s = [jnp.dot(kj[hh], qt_scr[hh, :, cs], preferred_element_type=F32) for hh, cs in groups]
        if biased:
            s = [sg + bias_of(hh, cs, nkeys) for sg, (hh, cs) in zip(s, groups)]
        p, alpha, m_new = [], [], []
        for sg, (hh, cs) in zip(s, groups):
            m_old = m_scr[hh, :, cs]
            m_cur = jnp.max(sg, axis=0, keepdims=True)
            if not biased:
                m_cur = m_cur + far[hh]
            mn = jnp.maximum(m_old, m_cur)
            alpha.append(jnp.exp2(m_old - mn))
            p.append(jnp.exp2(sg - (mn if biased else mn - far[hh])).astype(BF16))
            m_new.append(mn)
        pv = [jnp.dot(vtj[hh], pg, preferred_element_type=F32) for pg, (hh, cs) in zip(p, groups)]
        for x, (hh, cs) in enumerate(groups):
            acc_scr[hh, :, cs] = alpha[x] * acc_scr[hh, :, cs] + pv[x]
            m_scr[hh, :, cs] = m_new[x]

    n_far = jnp.maximum(i - 1, 0)

    def far_pair(jj, carry):
        tile(pl.multiple_of(jj * 2 * blk, 2 * blk), 2 * blk, False)
        return carry

    lax.fori_loop(0, n_far // 2, far_pair, 0)

    @pl.when(n_far % 2 == 1)
    def _():
        tile(pl.multiple_of((n_far - 1) * blk, blk), blk, False)

    @pl.when(i >= 1)
    def _():
        tile(pl.multiple_of((i - 1) * blk, blk), 2 * blk, True)

    @pl.when(i == 0)
    def _():
        tile(0, blk, True)

    lam = _lambda(lam_ref, lam_init)
    for hh in range(hp):
        acc = acc_scr[hh]
        wt = acc[:DV_B] / acc[DV_B:DV_B + 1]
        ot = wt[:, :blk] - lam * wt[:, blk:]
        ot = ot * lax.rsqrt(jnp.mean(ot * ot, axis=0, keepdims=True) + 1e-5)
        o_ref[0, :, hh * DV_B:(hh + 1) * DV_B] = (ot.T * sw_ref[...] * (1.0 - lam_init)).astype(o_ref.dtype)


def _attn_prompt(q, k16, vt16, rel_bias, lam_q, subln_w, lam_init, *, blk=512, hp=2):
    B, T, _ = q.shape
    blk = min(blk, T)
    bias = _prompt_bias_tiles(rel_bias, blk)
    far = rel_bias[NUM_BUCKETS - 1].astype(F32)
    vrows = vt16.shape[2]
    return pl.pallas_call(
        functools.partial(_attn_prompt_body, blk=blk, hp=hp, lam_init=lam_init),
        grid=(B, H_B // hp, T // blk),
        in_specs=[pl.BlockSpec(memory_space=pltpu.SMEM),
                  pl.BlockSpec((1, blk, hp * 2 * DK_B), lambda b, h, i: (b, i, h)),
                  pl.BlockSpec((1, T, hp * 2 * DK_B), lambda b, h, i: (b, 0, h)),
                  pl.BlockSpec((1, hp, vrows, T), lambda b, h, i: (b, h, 0, 0)),
                  pl.BlockSpec((hp, 2, blk, blk), lambda b, h, i: (h, 0, 0, 0)),
                  pl.BlockSpec((4, DK_B), lambda b, h, i: (0, 0)),
                  pl.BlockSpec((1, DV_B), lambda b, h, i: (0, 0))],
        out_specs=pl.BlockSpec((1, blk, hp * DV_B), lambda b, h, i: (b, i, h)),
        scratch_shapes=[pltpu.VMEM((hp, 2 * DK_B, 2 * blk), BF16), pltpu.VMEM((hp, 1, 2 * blk), F32),
                        pltpu.VMEM((hp, vrows, 2 * blk), F32)],
        out_shape=jax.ShapeDtypeStruct((B, T, H_B * DV_B), BF16),
        compiler_params=_params(("parallel", "parallel", "arbitrary")),
        name="attn_prompt",
    )(far, q, k16, vt16, bias, lam_q.astype(F32), subln_w.reshape(1, DV_B).astype(F32))


def _attn_sample_body(*refs, pp, t_new, lam_init):
    pt_ref, q_ref = refs[:2]
    k_refs = refs[2:2 + pp]
    v_refs = refs[2 + pp:2 + 2 * pp]
    kn_ref, vn_ref, bias_ref, lam_ref, sw_ref, o_ref, m_scr, l_scr, acc_scr = refs[2 + 2 * pp:]
    step = pl.program_id(1)
    last = step == pl.num_programs(1) - 1
    rows = 2 * t_new
    flat = PAGE_SIZE * H_B

    @pl.when(step == 0)
    def _():
        m_scr[...] = jnp.full(m_scr.shape, NEG_INF, F32)
        l_scr[...] = jnp.zeros(l_scr.shape, F32)
        acc_scr[...] = jnp.zeros(acc_scr.shape, F32)

    qall = jnp.concatenate([_stack_maps(q_ref[0, :, h * 2 * DK_B:(h + 1) * 2 * DK_B]) for h in range(H_B)],
                           axis=0).astype(BF16)

    def update(scores, values):
        m_old = m_scr[...]
        m_cur = jnp.max(scores[0], axis=1, keepdims=True)
        for s in scores[1:]:
            m_cur = jnp.maximum(m_cur, jnp.max(s, axis=1, keepdims=True))
        m_new = jnp.maximum(m_old, m_cur)
        alpha = jnp.exp2(m_old - m_new)
        l_new = alpha * l_scr[...]
        pv = None
        for s, v in zip(scores, values):
            p = jnp.exp2(s - m_new)
            l_new = l_new + jnp.sum(p, axis=1, keepdims=True)
            part = jnp.dot(p.astype(BF16), v, preferred_element_type=F32)
            pv = part if pv is None else pv + part
        l_scr[...] = l_new
        acc_scr[...] = alpha * acc_scr[...] + pv
        m_scr[...] = m_new

    def flat_page(ref):
        return ref[0].reshape(flat, 2 * DK_B).astype(BF16)

    scores = []
    for r in range(pp):
        bias = bias_ref[last.astype(jnp.int32)] if r == pp - 1 else bias_ref[0]
        scores.append(_dot_nt(qall, flat_page(k_refs[r])) + bias)
    update(scores, [flat_page(v_refs[r]) for r in range(pp)])

    @pl.when(last)
    def _():
        update([_dot_nt(qall, flat_page(kn_ref)) + bias_ref[2]], [flat_page(vn_ref)])
        w = acc_scr[...] / l_scr[...]
        lam = _lambda(lam_ref, lam_init)
        for h in range(H_B):
            o = w[h * rows:h * rows + t_new] - lam * w[h * rows + t_new:(h + 1) * rows]
            o = o * lax.rsqrt(jnp.mean(o * o, -1, keepdims=True) + 1e-5) * sw_ref[...] * (1.0 - lam_init)
            o_ref[0, :, h * DV_B:(h + 1) * DV_B] = o.astype(o_ref.dtype)


def _attn_sample(q, k_new, v_new, cache_k, cache_v, page_table, rel_bias, lam_q, subln_w, lam_init, *, pp=16):
    Bd, t_new, _ = q.shape
    n_pages = page_table.shape[1]
    assert n_pages % pp == 0
    bias = _sample_bias_tiles(rel_bias, t_new)
    padt = lambda a: jnp.pad(a, ((0, 0), (0, PAGE_SIZE - t_new), (0, 0), (0, 0)))
    page_spec = lambda r: pl.BlockSpec((1, PAGE_SIZE, H_B, 2 * DK_B),
                                       lambda b, s, pt, r=r: (pt[b, s * pp + r], 0, 0, 0))
    own_spec = pl.BlockSpec((1, PAGE_SIZE, H_B, 2 * DK_B), lambda b, s, pt: (b, 0, 0, 0))
    const = lambda *shape: pl.BlockSpec(shape, lambda b, s, pt: (0,) * len(shape))
    rows = H_B * 2 * t_new
    grid_spec = pltpu.PrefetchScalarGridSpec(
        num_scalar_prefetch=1,
        grid=(Bd, n_pages // pp),
        in_specs=([pl.BlockSpec((1, t_new, H_B * 2 * DK_B), lambda b, s, pt: (b, 0, 0))]
                  + [page_spec(r) for r in range(pp)] + [page_spec(r) for r in range(pp)]
                  + [own_spec, own_spec, const(3, rows, PAGE_SIZE * H_B), const(4, DK_B), const(1, DV_B)]),
        out_specs=pl.BlockSpec((1, t_new, H_B * DV_B), lambda b, s, pt: (b, 0, 0)),
        scratch_shapes=[pltpu.VMEM((rows, 1), F32), pltpu.VMEM((rows, 1), F32), pltpu.VMEM((rows, DV_B), F32)],
    )
    return pl.pallas_call(
        functools.partial(_attn_sample_body, pp=pp, t_new=t_new, lam_init=lam_init),
        grid_spec=grid_spec,
        out_shape=jax.ShapeDtypeStruct((Bd, t_new, H_B * DV_B), BF16),
        compiler_params=_params(("parallel", "arbitrary")),
        name="attn_sample",
    )(page_table, q, *([cache_k] * pp), *([cache_v] * pp), padt(k_new), padt(v_new), bias,
      lam_q.astype(F32), subln_w.reshape(1, DV_B).astype(F32))


def _gdn_mixer(x2d, B, T, w_qkv, w_zba, conv_w, a_log, dt_bias, gnorm_w, conv_state, ssm_state):
    state8 = jnp.pad(conv_state.astype(F32), ((0, 0), (8 - (CONV_W - 1), 0), (0, 0)))
    fused_tile = 512
    if T % fused_tile == 0:
        qkvc, tail8 = _inproj_conv(x2d, w_qkv, state8, conv_w.astype(F32), T, tm=fused_tile)
        qkvc = qkvc.reshape(B, T, CONV_DIM)
        xpad_tail = tail8[:, 8 - (CONV_W - 1):]
    else:
        proj = _mm(x2d, w_qkv).reshape(B, T, CONV_DIM)
        qkvc = _gdn_prep(proj, state8, conv_w.astype(F32), tt=fused_tile)
        xpad_tail = jnp.concatenate([conv_state.astype(F32), proj], axis=1)[:, -(CONV_W - 1):]
    zba2d = _mm(x2d, w_zba, tm=512)
    proj = zba2d.reshape(B, T, -1)
    bgc = _gdn_gates(zba2d, VDIM_A // LANES, a_log, dt_bias, min(CHUNK, T)).reshape(B, T, LANES)
    if T % CHUNK:
        tp = -(-T // CHUNK) * CHUNK
        padt = lambda a: jnp.pad(a, ((0, 0), (0, tp - T), (0, 0)))
        lane = jnp.arange(LANES)
        is_gc = (lane >= 2 * H_V) & (lane < 3 * H_V)
        tail = jnp.broadcast_to(jnp.where(is_gc, bgc[:, T - 1:T], 0.0), (B, tp - T, LANES))
        bgc = jnp.concatenate([bgc, tail], axis=1)
        qkvc = padt(qkvc)
        z_arr = padt(proj[:, :, :VDIM_A])
    else:
        z_arr = proj
    u, w, qg, kd, aqk, dec = _gdn_intra(qkvc, bgc)
    y, s_new = _gdn_state(u, w, qg, kd, aqk, dec, z_arr, 0, ssm_state, gnorm_w)
    return y[:, :T].reshape(B * T, VDIM_A), xpad_tail, s_new


def _mlp_and_embed(x, p2d, ln_g, ln_b, w_up, w_down, w_ple, w_pg):
    return _mlp_gate(x, p2d, w_up, w_down, ln_g, ln_b, w_ple, w_pg)


def kernel(x_prompt, x_sample, state_conv, state_ssm, cache_k, cache_v, page_table, p_prompt, p_sample,
           w_in_a, conv_w_a, a_log_a, dt_bias_a, gnorm_w_a, w_out_a, w_kv, w_q_b, lam_b, subln_w_b, w_o_b,
           rel_bias, ln_g, ln_b, w_up, w_down, w_ple, w_pg):
    bf = lambda w: w.astype(BF16)
    w_qkv = bf(w_in_a[:, :, :CONV_DIM])
    w_zba = bf(jnp.pad(w_in_a[:, :, CONV_DIM:], ((0, 0), (0, 0), (0, IN_A_PAD - IN_A))))
    w_out, w_kv16, w_q, w_o = bf(w_out_a), bf(w_kv), bf(w_q_b), bf(w_o_b)
    w_up16, w_down16, w_ple16, w_pg16 = bf(w_up), bf(w_down), bf(w_ple), bf(w_pg)

    def run(x3d, p, conv_state, ssm_state, attn_fn):
        B, T, _ = x3d.shape
        x = x3d.reshape(B * T, D_MODEL)
        convs, ssms = [], []
        k_sh = v_sh = kv16 = None
        for i in range(DEPTH):
            if i < N_A:
                y, cs, ss = _gdn_mixer(x, B, T, w_qkv[i], w_zba[i], conv_w_a[i], a_log_a[i], dt_bias_a[i],
                                       gnorm_w_a[i], conv_state[i], ssm_state[i])
                convs.append(cs)
                ssms.append(ss)
                x = _mm_ln(y, w_out[i], x, ln_g[i, 0], ln_b[i, 0])
            else:
                j = i - N_A
                lam_init = 0.8 - 0.6 * math.exp(-0.3 * i)
                q = _mm(x, w_q[j]).reshape(B, T, H_B * 2 * DK_B)
                o = attn_fn(q, k_sh, v_sh, kv16, lam_b[j], subln_w_b[j], lam_init)
                x = _mm_ln(o.reshape(B * T, H_B * DV_B), w_o[j], x, ln_g[i, 0], ln_b[i, 0])
            x = _mlp_and_embed(x, p[i].reshape(B * T, -1), ln_g[i, 1], ln_b[i, 1], w_up16[i], w_down16[i],
                               w_ple16[i], w_pg16[i])
            if i == N_A - 1:
                k_sh, v_sh, k16, vt16 = _kv_proj(x, w_kv16, T)
                k_sh = k_sh.reshape(B, T, H_B, 2 * DK_B)
                v_sh = v_sh.reshape(B, T, H_B, DV_B)
                kv16 = (k16.reshape(B, T, -1), vt16)
        return x.reshape(B, T, D_MODEL), jnp.stack(convs), jnp.stack(ssms), k_sh, v_sh

    def attn_prompt(q, k_sh, v_sh, kv16, lam_q, sw, lam_init):
        return _attn_prompt(q, kv16[0], kv16[1], rel_bias, lam_q, sw, lam_init)

    def attn_sample(q, k_sh, v_sh, kv16, lam_q, sw, lam_init):
        return _attn_sample(q, k_sh, v_sh, cache_k, cache_v, page_table, rel_bias, lam_q, sw, lam_init)

    Bp = x_prompt.shape[0]
    conv0 = jnp.zeros((N_A, Bp, CONV_W - 1, CONV_DIM), x_prompt.dtype)
    ssm0 = jnp.zeros((N_A, Bp, H_V, DK_A, DV_A), state_ssm.dtype)
    y_prompt, conv_p, ssm_p, k_p, v_p = run(x_prompt, p_prompt, conv0, ssm0, attn_prompt)
    y_sample, conv_s, ssm_s, k_s, v_s = run(x_sample, p_sample, state_conv, state_ssm, attn_sample)
    return (y_prompt, y_sample, conv_p, ssm_p, k_p, v_p, conv_s, ssm_s, k_s, v_s)
```

```python
import functools
import math

import jax
import jax.numpy as jnp
from jax import lax
from jax.experimental import pallas as pl
from jax.experimental.pallas import tpu as pltpu

F32 = jnp.float32
BF16 = jnp.bfloat16

D_MODEL = 1024
DEPTH = 2
N_A = DEPTH // 2
PAGE_SIZE = 128
H_QK = 8
H_V = 16
DK_A = 128
DV_A = 128
CONV_W = 4
CHUNK = 64
QK_DIM = H_QK * DK_A
VDIM_A = H_V * DV_A
CONV_DIM = 2 * QK_DIM + VDIM_A
IN_A = CONV_DIM + VDIM_A + 2 * H_V
IN_A_PAD = CONV_DIM + VDIM_A + 128
H_B = 8
DK_B = 64
DV_B = 128
NUM_BUCKETS = 32
MAX_DISTANCE = 128
D_FF = 4 * D_MODEL
ALPHA = (2 * DEPTH) ** 0.25

LANES = 128
VMEM_LIMIT = 56 * 1024 * 1024
NEG_INF = float("-inf")
LOG2E = math.log2(math.e)


def _params(sem, vmem=VMEM_LIMIT):
    return pltpu.CompilerParams(dimension_semantics=sem, vmem_limit_bytes=vmem)


def _sigmoid(x):
    return 1.0 / (1.0 + jnp.exp(-x))


def _dot_nt(a, b):
    return lax.dot_general(a, b, (((1,), (1,)), ((), ())), preferred_element_type=F32)


def _dot_tn(a, b):
    return lax.dot_general(a, b, (((0,), (0,)), ((), ())), preferred_element_type=F32)


def _mm_body(x_ref, w_ref, *o_refs, act):
    acc = jnp.dot(x_ref[...].astype(BF16), w_ref[...], preferred_element_type=F32)
    if act == "relu2":
        r = jnp.maximum(acc, 0.0)
        acc = r * r
    for o_ref in o_refs:
        o_ref[...] = acc.astype(o_ref.dtype)


def _mm(x, w, *, act=None, out_dtypes=(F32,), tm=1024, tn=None):
    M, K = x.shape
    N = w.shape[1]
    tm = min(tm, M)
    tn = N if tn is None else tn
    assert M % tm == 0 and N % tn == 0
    outs = pl.pallas_call(
        functools.partial(_mm_body, act=act),
        grid=(M // tm, N // tn),
        in_specs=[pl.BlockSpec((tm, K), lambda i, j: (i, 0)),
                  pl.BlockSpec((K, tn), lambda i, j: (0, j))],
        out_specs=[pl.BlockSpec((tm, tn), lambda i, j: (i, j)) for _ in out_dtypes],
        out_shape=[jax.ShapeDtypeStruct((M, N), dt) for dt in out_dtypes],
        compiler_params=_params(("parallel", "arbitrary")),
        name="mm",
    )(x, w)
    return outs[0] if len(out_dtypes) == 1 else outs


V_ONES_ROWS = 16


def _kv_proj_body(x_ref, w_ref, k_ref, v_ref, k16_ref, vt16_ref):
    acc = jnp.dot(x_ref[...].astype(BF16), w_ref[...], preferred_element_type=F32)
    nk = H_B * 2 * DK_B
    k_ref[...] = pltpu.einshape("m(hd)->mhd", acc[:, :nk], h=H_B)
    v_ref[...] = pltpu.einshape("m(hd)->mhd", acc[:, nk:], h=H_B)
    k16_ref[...] = acc[:, :nk].astype(BF16)
    for h in range(H_B):
        vt16_ref[0, h, :DV_B, :] = acc[:, nk + h * DV_B:nk + (h + 1) * DV_B].T.astype(BF16)
        vt16_ref[0, h, DV_B:, :] = jnp.ones((V_ONES_ROWS, acc.shape[0]), BF16)


def _kv_proj(x, w_kv, T, *, tm=512):
    M, K = x.shape
    N = w_kv.shape[1]
    tm = min(tm, T)
    assert T % tm == 0
    tps = T // tm
    nk = H_B * 2 * DK_B
    return pl.pallas_call(
        _kv_proj_body,
        grid=(M // tm,),
        in_specs=[pl.BlockSpec((tm, K), lambda i: (i, 0)),
                  pl.BlockSpec((K, N), lambda i: (0, 0), pipeline_mode=pl.Buffered(1))],
        out_specs=[pl.BlockSpec((tm, H_B, 2 * DK_B), lambda i: (i, 0, 0)),
                   pl.BlockSpec((tm, H_B, DV_B), lambda i: (i, 0, 0)),
                   pl.BlockSpec((tm, nk), lambda i: (i, 0)),
                   pl.BlockSpec((1, H_B, DV_B + V_ONES_ROWS, tm), lambda i: (i // tps, 0, 0, i % tps))],
        out_shape=[jax.ShapeDtypeStruct((M, H_B, 2 * DK_B), F32), jax.ShapeDtypeStruct((M, H_B, DV_B), F32),
                   jax.ShapeDtypeStruct((M, nk), BF16),
                   jax.ShapeDtypeStruct((M // T, H_B, DV_B + V_ONES_ROWS, T), BF16)],
        compiler_params=_params(("parallel",)),
        name="kv_proj",
    )(x, w_kv)


def _layernorm_rows(y, g, b):
    mu = jnp.mean(y, -1, keepdims=True)
    yc = y - mu
    var = jnp.mean(yc * yc, -1, keepdims=True)
    return yc * lax.rsqrt(var + 1e-5) * g + b


def _layer_tail_body(x_ref, y_ref, p_ref, wo_ref, g1_ref, b1_ref, wup_ref, wdown_ref, g2_ref, b2_ref,
                     wple_ref, wpg_ref, o_ref):
    x1 = _layernorm_rows(ALPHA * x_ref[...] + jnp.dot(y_ref[...], wo_ref[...], preferred_element_type=F32),
                         g1_ref[...], b1_ref[...])
    u = jnp.maximum(jnp.dot(x1.astype(BF16), wup_ref[...], preferred_element_type=F32), 0.0)
    h = (u * u).astype(BF16)
    x2 = _layernorm_rows(ALPHA * x1 + jnp.dot(h, wdown_ref[...], preferred_element_type=F32),
                         g2_ref[...], b2_ref[...])
    e = jnp.dot(p_ref[...].astype(BF16), wple_ref[...], preferred_element_type=F32)
    gt = jnp.dot(x2.astype(BF16), wpg_ref[...], preferred_element_type=F32)
    o_ref[...] = x2 + e * _sigmoid(gt)


def _layer_tail(x, y, p, w_o, g1, b1, w_up, w_down, g2, b2, w_ple, w_pg, *, tm=512):
    M, D = x.shape
    Ky = y.shape[1]
    P = p.shape[1]
    F = w_up.shape[1]
    tm = min(tm, M)
    resident = lambda r, c: pl.BlockSpec((r, c), lambda i: (0, 0), pipeline_mode=pl.Buffered(1))
    row = lambda v: v.reshape(1, D)
    return pl.pallas_call(
        _layer_tail_body,
        grid=(M // tm,),
        in_specs=[pl.BlockSpec((tm, D), lambda i: (i, 0)),
                  pl.BlockSpec((tm, Ky), lambda i: (i, 0)),
                  pl.BlockSpec((tm, P), lambda i: (i, 0)),
                  resident(Ky, D), resident(1, D), resident(1, D),
                  resident(D, F), resident(F, D), resident(1, D), resident(1, D), resident(P, D), resident(D, D)],
        out_specs=pl.BlockSpec((tm, D), lambda i: (i, 0)),
        out_shape=jax.ShapeDtypeStruct((M, D), F32),
        compiler_params=_params(("parallel",)),
        name="layer_tail",
    )(x, y, p, w_o, row(g1), row(b1), w_up, w_down, row(g2), row(b2), w_ple, w_pg)


def _conv_silu_norm(xf_ref, cw_ref, store, first_group):
    tt = xf_ref.shape[0] - 8
    for h in range(xf_ref.shape[1] // LANES):
        sl = slice(h * LANES, (h + 1) * LANES)
        cw = cw_ref[:, sl]
        c = xf_ref[8:, sl] * cw[3:4]
        for s in (1, 2, 3):
            c = c + xf_ref[8 - s:8 - s + tt, sl] * cw[3 - s:4 - s]
        c = c * _sigmoid(c)
        group = first_group + h
        nrm = lax.rsqrt(jnp.sum(c * c, -1, keepdims=True) + 1e-6)
        scale = jnp.where(group < H_QK, nrm * (DK_A ** -0.5), jnp.where(group < 2 * H_QK, nrm, 1.0))
        store(sl, c * scale)


def _gdn_prep_body(x_ref, halo_ref, st_ref, cw_ref, o_ref, xf_scr, *, gw, nb):
    t = pl.program_id(1)
    j = pl.program_id(2)
    for bi in range(nb):
        xf_scr[:8] = jnp.where(t == 0, st_ref[bi], halo_ref[bi])
        xf_scr[8:] = x_ref[bi]

        def store(sl, val, bi=bi):
            o_ref[bi, :, sl] = val

        _conv_silu_norm(xf_scr, cw_ref, store, j * (gw // LANES))


def _inproj_conv_body(x_ref, w_ref, st_ref, cw_ref, o_ref, tail_ref, carry_scr, xf_scr, *, tiles_per_seq):
    i = pl.program_id(0)
    j = pl.program_id(1)
    tn = w_ref.shape[1]
    xf_scr[:8] = jnp.where(i % tiles_per_seq == 0, st_ref[0], carry_scr[j])
    xf_scr[8:] = jnp.dot(x_ref[...].astype(BF16), w_ref[...], preferred_element_type=F32)

    def store(sl, val):
        o_ref[:, sl] = val

    _conv_silu_norm(xf_scr, cw_ref, store, j * (tn // LANES))
    last8 = xf_scr[xf_scr.shape[0] - 8:]
    carry_scr[j] = last8
    tail_ref[0] = last8


def _inproj_conv(x2d, w_qkv, state8, conv_w, T, *, tm=512, tn=1024):
    M, K = x2d.shape
    assert T % tm == 0 and CONV_DIM % tn == 0
    tps = T // tm
    qkvc, tails = pl.pallas_call(
        functools.partial(_inproj_conv_body, tiles_per_seq=tps),
        grid=(M // tm, CONV_DIM // tn),
        in_specs=[pl.BlockSpec((tm, K), lambda i, j: (i, 0)),
                  pl.BlockSpec((K, tn), lambda i, j: (0, j)),
                  pl.BlockSpec((1, 8, tn), lambda i, j: (i // tps, 0, j)),
                  pl.BlockSpec((CONV_W, tn), lambda i, j: (0, j))],
        out_specs=[pl.BlockSpec((tm, tn), lambda i, j: (i, j)),
                   pl.BlockSpec((1, 8, tn), lambda i, j: (i, 0, j))],
        out_shape=[jax.ShapeDtypeStruct((M, CONV_DIM), F32), jax.ShapeDtypeStruct((M // tm, 8, CONV_DIM), F32)],
        scratch_shapes=[pltpu.VMEM((CONV_DIM // tn, 8, tn), F32), pltpu.VMEM((tm + 8, tn), F32)],
        compiler_params=_params(("arbitrary", "arbitrary")),
        name="inproj_conv",
    )(x2d, w_qkv, state8, conv_w)
    return qkvc, tails[tps - 1::tps]


def _gdn_prep(proj, state8, conv_w, *, tt, gw=512):
    B, T, _ = proj.shape
    tt = min(tt, T)
    hb = tt // 8
    nb = min(max(64 // tt, 1), B)
    assert B % nb == 0
    return pl.pallas_call(
        functools.partial(_gdn_prep_body, gw=gw, nb=nb),
        grid=(B // nb, T // tt, CONV_DIM // gw),
        in_specs=[pl.BlockSpec((nb, tt, gw), lambda b, t, j: (b, t, j)),
                  pl.BlockSpec((nb, 8, gw), lambda b, t, j: (b, jnp.maximum(t * hb - 1, 0), j)),
                  pl.BlockSpec((nb, 8, gw), lambda b, t, j: (b, 0, j)),
                  pl.BlockSpec((CONV_W, gw), lambda b, t, j: (0, j))],
        out_specs=pl.BlockSpec((nb, tt, gw), lambda b, t, j: (b, t, j)),
        out_shape=jax.ShapeDtypeStruct((B, T, CONV_DIM), F32),
        scratch_shapes=[pltpu.VMEM((tt + 8, gw), F32)],
        compiler_params=_params(("parallel", "parallel", "parallel")),
        name="gdn_prep",
    )(proj, proj, state8, conv_w)


def _gdn_gates_body(x_ref, alog_ref, dtb_ref, o_ref, *, period):
    ba = x_ref[...]
    lane = lax.broadcasted_iota(jnp.int32, ba.shape, 1)
    beta = _sigmoid(ba)
    xx = ba + dtb_ref[...]
    softplus = jnp.maximum(xx, 0.0) + jnp.log(1.0 + jnp.exp(-jnp.abs(xx)))
    g = -jnp.exp(alog_ref[...]) * softplus
    pos = lax.broadcasted_iota(jnp.int32, ba.shape, 0) % period
    gc = g
    s = 1
    while s < period:
        gc = gc + jnp.where(pos >= s, pltpu.roll(gc, s, 0), 0.0)
        s *= 2
    gc = pltpu.roll(gc, H_V, 1)
    o_ref[...] = jnp.where(lane < H_V, beta,
                           jnp.where(lane < 2 * H_V, g, jnp.where(lane < 3 * H_V, gc, 0.0)))


def _gdn_gates(proj2d, col, a_log, dt_bias, period, *, tm=1024):
    M = proj2d.shape[0]
    tm = min(tm, M)
    assert tm % period == 0
    pad = lambda v: jnp.pad(v.astype(F32), (H_V, LANES - 2 * H_V)).reshape(1, LANES)
    return pl.pallas_call(
        functools.partial(_gdn_gates_body, period=period),
        grid=(M // tm,),
        in_specs=[pl.BlockSpec((tm, LANES), lambda i: (i, col)),
                  pl.BlockSpec((1, LANES), lambda i: (0, 0)),
                  pl.BlockSpec((1, LANES), lambda i: (0, 0))],
        out_specs=pl.BlockSpec((tm, LANES), lambda i: (i, 0)),
        out_shape=jax.ShapeDtypeStruct((M, LANES), F32),
        compiler_params=_params(("parallel",)),
        name="gdn_gates",
    )(proj2d, pad(a_log), pad(dt_bias))


def _gdn_intra_body(q_ref, k_ref, v_ref, bgc_ref, gct_ref, u_ref, w_ref, qg_ref, kd_ref, aqk_ref, dec_ref,
                    *, nb, ca):
    hk = pl.program_id(2)
    C = CHUNK
    row = lax.broadcasted_iota(jnp.int32, (C, C), 0)
    col = lax.broadcasted_iota(jnp.int32, (C, C), 1)
    incl = row >= col
    strict = row > col
    eye = (row == col).astype(F32)
    lane = lax.broadcasted_iota(jnp.int32, (C, LANES), 1)
    blocks = [(bi, ci) for bi in range(nb) for ci in range(ca)]
    at = lambda x: (blocks[x][0], slice(blocks[x][1] * C, (blocks[x][1] + 1) * C))
    full = (slice(None),)

    def setup(x):
        bi, ci = blocks[x]
        k16 = k_ref[at(x) + full].astype(BF16)
        kk = _dot_nt(k16, k16)
        qk = _dot_nt(q_ref[at(x) + full].astype(BF16), k16)
        bgc = bgc_ref[at(x) + full]
        out = []
        for e in range(2):
            hv = 2 * hk + e
            b = jnp.sum(jnp.where(lane == hv, bgc, 0.0), axis=1, keepdims=True)
            g = jnp.sum(jnp.where(lane == 2 * H_V + hv, bgc, 0.0), axis=1, keepdims=True)
            dm = jnp.exp(jnp.where(incl, g - gct_ref[bi, ci, pl.ds(hv, 1), :], NEG_INF))
            aqk_ref[at(x) + (slice(e * C, (e + 1) * C),)] = jnp.where(incl, qk * dm, 0.0).astype(BF16)
            out.append((x, e, b, g, jnp.where(strict, -(b * kk * dm), 0.0)))
        return out

    def finish(chains, tinv):
        eg = [jnp.exp(g) for (_, _, _, g, _) in chains]
        rhs = []
        for (x, e, b, g, _), egy in zip(chains, eg):
            k = k_ref[at(x) + full]
            v = v_ref[at(x) + (slice(e * DV_A, (e + 1) * DV_A),)]
            rhs.append(jnp.concatenate([v * b, k * (b * egy)], axis=1).astype(BF16))
        uw = [jnp.dot(t.astype(BF16), r, preferred_element_type=F32) for t, r in zip(tinv, rhs)]
        for (x, e, b, g, _), egy, uwy in zip(chains, eg, uw):
            bi, ci = blocks[x]
            hs = at(x) + (slice(e * DV_A, (e + 1) * DV_A),)
            u_ref[hs] = uwy[:, :DV_A].astype(BF16)
            w_ref[hs] = uwy[:, DV_A:].astype(BF16)
            qg_ref[hs] = (q_ref[at(x) + full] * egy).astype(BF16)
            g_last = g[C - 1:C, :]
            kd_ref[hs] = (k_ref[at(x) + full] * jnp.exp(g_last - g)).astype(BF16)
            dec_ref[bi, 0, ci, e:e + 1, :] = jnp.broadcast_to(jnp.exp(g_last), (1, LANES))

    chains = [c for x in range(len(blocks)) for c in setup(x)]
    npow = [c[4] for c in chains]
    tinv = [eye + n for n in npow]
    for _ in range(5):
        n16 = [n.astype(BF16) for n in npow]
        npow = [jnp.dot(n, n, preferred_element_type=F32) for n in n16]
        tinv = [t + jnp.dot(t.astype(BF16), n.astype(BF16), preferred_element_type=F32)
                for t, n in zip(tinv, npow)]
    finish(chains, tinv)


def _gdn_intra(qkvc, bgc, *, chains=32):
    B, T, _ = qkvc.shape
    nc = T // CHUNK
    ca = min(chains // 2, nc)
    nb = min(max(chains // (2 * ca), 1), B)
    assert nc % ca == 0 and B % nb == 0
    ta = ca * CHUNK
    gct = bgc[..., 2 * H_V:3 * H_V].reshape(B, nc, CHUNK, H_V).transpose(0, 1, 3, 2)
    kb = QK_DIM // DK_A
    wide = lambda: pl.BlockSpec((nb, ta, 2 * DV_A), lambda b, t, h: (b, t, h))
    return pl.pallas_call(
        functools.partial(_gdn_intra_body, nb=nb, ca=ca),
        grid=(B // nb, T // ta, H_QK),
        in_specs=[pl.BlockSpec((nb, ta, DK_A), lambda b, t, h: (b, t, h)),
                  pl.BlockSpec((nb, ta, DK_A), lambda b, t, h: (b, t, kb + h)),
                  pl.BlockSpec((nb, ta, 2 * DV_A), lambda b, t, h: (b, t, kb + h)),
                  pl.BlockSpec((nb, ta, LANES), lambda b, t, h: (b, t, 0)),
                  pl.BlockSpec((nb, ca, H_V, CHUNK), lambda b, t, h: (b, t, 0, 0))],
        out_specs=[wide(), wide(), wide(), wide(),
                   pl.BlockSpec((nb, ta, 2 * CHUNK), lambda b, t, h: (b, t, h)),
                   pl.BlockSpec((nb, 1, ca, 2, LANES), lambda b, t, h: (b, h, t, 0, 0))],
        out_shape=[jax.ShapeDtypeStruct((B, T, VDIM_A), BF16)] * 4
        + [jax.ShapeDtypeStruct((B, T, H_V * CHUNK), BF16),
           jax.ShapeDtypeStruct((B, H_QK, nc, 2, LANES), F32)],
        compiler_params=_params(("parallel", "parallel", "arbitrary")),
        name="gdn_intra",
    )(qkvc, qkvc, qkvc, bgc, gct)


def _gdn_state_body(u_ref, w_ref, qg_ref, kd_ref, aqk_ref, dec_ref, z_ref, s0_ref, gw_ref, y_ref, st_ref, s_scr,
                    *, cb, hg):
    t = pl.program_id(2)
    C = CHUNK

    @pl.when(t == 0)
    def _():
        s_scr[...] = s0_ref[0]

    def chunk(ci, carry):
        rows = pl.ds(pl.multiple_of(ci * C, C), C)
        cols = [slice(hh * DV_A, (hh + 1) * DV_A) for hh in range(hg)]
        s = [s_scr[hh] for hh in range(hg)]
        r = [jnp.dot(jnp.concatenate([w_ref[0, rows, cs], qg_ref[0, rows, cs]], axis=0), s[hh].astype(BF16),
                     preferred_element_type=F32) for hh, cs in enumerate(cols)]
        v_new = [(u_ref[0, rows, cs].astype(F32) - r[hh][:C]).astype(BF16) for hh, cs in enumerate(cols)]
        s_upd = [_dot_tn(kd_ref[0, rows, cs], v_new[hh]) for hh, cs in enumerate(cols)]
        o_in = [jnp.dot(aqk_ref[0, rows, hh * C:(hh + 1) * C], v_new[hh], preferred_element_type=F32)
                for hh in range(hg)]
        for hh, cs in enumerate(cols):
            dec = dec_ref[0, hh // 2, pl.ds(ci, 1), hh % 2, :]
            s_scr[hh] = s[hh] * dec + s_upd[hh]
            o = r[hh][C:] + o_in[hh]
            z = z_ref[0, rows, cs]
            y = o * lax.rsqrt(jnp.mean(o * o, -1, keepdims=True) + 1e-6) * gw_ref[...] * (z * _sigmoid(z))
            y_ref[0, rows, cs] = y.astype(y_ref.dtype)
        return carry

    lax.fori_loop(0, cb, chunk, 0)

    @pl.when(t == pl.num_programs(2) - 1)
    def _():
        st_ref[0] = s_scr[...]


def _gdn_state(u, w, qg, kd, aqk, dec, z_arr, z_off, s0, gnorm_w, *, cb=8, hg=16):
    B, T, _ = u.shape
    nc = T // CHUNK
    cb = min(cb, nc)
    tb = cb * CHUNK
    ng = H_V // hg
    assert z_off % (hg * DV_A) == 0
    z_col = z_off // (hg * DV_A)
    wide = lambda: pl.BlockSpec((1, tb, hg * DV_A), lambda b, g, t: (b, t, g))
    return pl.pallas_call(
        functools.partial(_gdn_state_body, cb=cb, hg=hg),
        grid=(B, ng, T // tb),
        in_specs=[wide(), wide(), wide(), wide(),
                  pl.BlockSpec((1, tb, hg * CHUNK), lambda b, g, t: (b, t, g)),
                  pl.BlockSpec((1, hg // 2, cb, 2, LANES), lambda b, g, t: (b, g, t, 0, 0)),
                  pl.BlockSpec((1, tb, hg * DV_A), lambda b, g, t: (b, t, z_col + g)),
                  pl.BlockSpec((1, hg, DK_A, DV_A), lambda b, g, t: (b, g, 0, 0)),
                  pl.BlockSpec((1, DV_A), lambda b, g, t: (0, 0))],
        out_specs=[wide(), pl.BlockSpec((1, hg, DK_A, DV_A), lambda b, g, t: (b, g, 0, 0))],
        out_shape=[jax.ShapeDtypeStruct((B, T, VDIM_A), BF16),
                   jax.ShapeDtypeStruct((B, H_V, DK_A, DV_A), F32)],
        scratch_shapes=[pltpu.VMEM((hg, DK_A, DV_A), F32)],
        compiler_params=_params(("parallel", "parallel", "arbitrary")),
        name="gdn_state",
    )(u, w, qg, kd, aqk, dec, z_arr, s0, gnorm_w.reshape(1, DV_A).astype(F32))


def _bucket_starts():
    max_exact = NUM_BUCKETS // 2
    starts = list(range(max_exact + 1))
    n = max_exact
    for b in range(max_exact + 1, NUM_BUCKETS):
        while max_exact + int(math.log(n / max_exact) / math.log(MAX_DISTANCE / max_exact)
                              * (NUM_BUCKETS - max_exact)) < b:
            n += 1
        starts.append(n)
    return starts


_BUCKET_START = _bucket_starts()


def _bias_of_distance(n, tab_ref, h):
    val = jnp.full(n.shape, tab_ref[(NUM_BUCKETS - 1) * H_B + h], F32)
    for b in range(NUM_BUCKETS - 2, -1, -1):
        val = jnp.where(n < _BUCKET_START[b + 1], tab_ref[b * H_B + h], val)
    return val


def _prompt_bias_body(tab_ref, o_ref, *, blk):
    h = pl.program_id(0)
    key = lax.broadcasted_iota(jnp.int32, (blk, blk), 0)
    qry = lax.broadcasted_iota(jnp.int32, (blk, blk), 1)
    d = qry - key
    o_ref[0, 0] = LOG2E * _bias_of_distance(d + blk, tab_ref, h)
    o_ref[0, 1] = jnp.where(d >= 0, LOG2E * _bias_of_distance(jnp.maximum(d, 0), tab_ref, h), NEG_INF)


def _prompt_bias_tiles(rel_bias, blk):
    assert blk >= _BUCKET_START[NUM_BUCKETS - 1]
    return pl.pallas_call(
        functools.partial(_prompt_bias_body, blk=blk),
        grid=(H_B,),
        in_specs=[pl.BlockSpec(memory_space=pltpu.SMEM)],
        out_specs=pl.BlockSpec((1, 2, blk, blk), lambda h: (h, 0, 0, 0)),
        out_shape=jax.ShapeDtypeStruct((H_B, 2, blk, blk), F32),
        compiler_params=_params(("arbitrary",)),
        name="prompt_bias",
    )(rel_bias.astype(F32).reshape(-1))


def _sample_bias_body(tab_ref, o_ref, *, t_new):
    h = pl.program_id(0)
    shape = (2 * t_new, PAGE_SIZE * H_B)
    r = lax.broadcasted_iota(jnp.int32, shape, 0)
    j = lax.broadcasted_iota(jnp.int32, shape, 1)
    tok = jnp.where(r >= t_new, r - t_new, r)
    c = j // H_B
    same_head = (j % H_B) == h
    far = jnp.full(shape, LOG2E * tab_ref[(NUM_BUCKETS - 1) * H_B + h], F32)
    o_ref[0] = jnp.where(same_head, far, NEG_INF)
    o_ref[1] = jnp.where(same_head, LOG2E * _bias_of_distance(PAGE_SIZE + tok - c, tab_ref, h), NEG_INF)
    d = tok - c
    o_ref[2] = jnp.where(same_head & (d >= 0), LOG2E * _bias_of_distance(jnp.maximum(d, 0), tab_ref, h), NEG_INF)


def _sample_bias_tiles(rel_bias, t_new):
    rows = 2 * t_new
    return pl.pallas_call(
        functools.partial(_sample_bias_body, t_new=t_new),
        grid=(H_B,),
        in_specs=[pl.BlockSpec(memory_space=pltpu.SMEM)],
        out_specs=pl.BlockSpec((3, rows, PAGE_SIZE * H_B), lambda h: (0, h, 0)),
        out_shape=jax.ShapeDtypeStruct((3, H_B * rows, PAGE_SIZE * H_B), F32),
        compiler_params=_params(("arbitrary",)),
        name="sample_bias",
    )(rel_bias.astype(F32).reshape(-1))


def _lambda(lam_ref, lam_init):
    lq = lam_ref[...]
    a = jnp.sum(lq[0:1] * lq[1:2], axis=1, keepdims=True)
    b = jnp.sum(lq[2:3] * lq[3:4], axis=1, keepdims=True)
    return jnp.exp(a) - jnp.exp(b) + lam_init


def _stack_maps(q):
    lane = lax.broadcasted_iota(jnp.int32, q.shape, 1)
    q = q * (DK_B ** -0.5 * LOG2E)
    return jnp.concatenate([jnp.where(lane < DK_B, q, 0.0), jnp.where(lane >= DK_B, q, 0.0)], axis=0)


def _attn_prompt_body(far_ref, q_ref, k_ref, vt_ref, bias_ref, lam_ref, sw_ref, o_ref,
                      qt_scr, m_scr, acc_scr, *, blk, hp, lam_init):
    h0 = pl.program_id(1) * hp
    i = pl.program_id(2)
    hd = 2 * DK_B
    for hh in range(hp):
        qt_scr[hh] = _stack_maps(q_ref[0, :, hh * hd:(hh + 1) * hd]).T.astype(BF16)
    m_scr[...] = jnp.full(m_scr.shape, NEG_INF, F32)
    acc_scr[...] = jnp.zeros(acc_scr.shape, F32)
    far = [far_ref[h0 + hh] * LOG2E for hh in range(hp)]
    gw = min(blk, 2 * LANES)
    groups = [(hh, slice(g * gw, (g + 1) * gw)) for hh in range(hp) for g in range(2 * blk // gw)]

    def bias_of(hh, cs, nkeys):
        c0 = cs.start % blk
        if nkeys == 2 * blk:
            return bias_ref[hh, :, :, c0:c0 + gw].reshape(2 * blk, gw)
        return bias_ref[hh, 1, :, c0:c0 + gw]

    def tile(start, nkeys, biased):
        kj = [k_ref[0, pl.ds(start, nkeys), hh * hd:(hh + 1) * hd] for hh in range(hp)]
        vtj = [vt_ref[0, hh, :, pl.ds(start, nkeys)] for hh in range(hp)]
        s = [jnp.dot(kj[hh], qt_scr[hh, :, cs], preferred_element_type=F32) for hh, cs in groups]
        if biased:
            s = [sg + bias_of(hh, cs, nkeys) for sg, (hh, cs) in zip(s, groups)]
        p, alpha, m_new = [], [], []
        for sg, (hh, cs) in zip(s, groups):
            m_old = m_scr[hh, :, cs]
            m_cur = jnp.max(sg, axis=0, keepdims=True)
            if not biased:
                m_cur = m_cur + far[hh]
            mn = jnp.maximum(m_old, m_cur)
            alpha.append(jnp.exp2(m_old - mn))
            p.append(jnp.exp2(sg - (mn if biased else mn - far[hh])).astype(BF16))
            m_new.append(mn)
        pv = [jnp.dot(vtj[hh], pg, preferred_element_type=F32) for pg, (hh, cs) in zip(p, groups)]
        for x, (hh, cs) in enumerate(groups):
            acc_scr[hh, :, cs] = alpha[x] * acc_scr[hh, :, cs] + pv[x]
            m_scr[hh, :, cs] = m_new[x]

    n_far = jnp.maximum(i - 1, 0)

    def far_pair(jj, carry):
        tile(pl.multiple_of(jj * 2 * blk, 2 * blk), 2 * blk, False)
        return carry

    lax.fori_loop(0, n_far // 2, far_pair, 0)

    @pl.when(n_far % 2 == 1)
    def _():
        tile(pl.multiple_of((n_far - 1) * blk, blk), blk, False)

    @pl.when(i >= 1)
    def _():
        tile(pl.multiple_of((i - 1) * blk, blk), 2 * blk, True)

    @pl.when(i == 0)
    def _():
        tile(0, blk, True)

    lam = _lambda(lam_ref, lam_init)
    for hh in range(hp):
        acc = acc_scr[hh]
        wt = acc[:DV_B] / acc[DV_B:DV_B + 1]
        ot = wt[:, :blk] - lam * wt[:, blk:]
        ot = ot * lax.rsqrt(jnp.mean(ot * ot, axis=0, keepdims=True) + 1e-5)
        o_ref[0, :, hh * DV_B:(hh + 1) * DV_B] = (ot.T * sw_ref[...] * (1.0 - lam_init)).astype(o_ref.dtype)


def _attn_prompt(q, k16, vt16, rel_bias, lam_q, subln_w, lam_init, *, blk=512, hp=2):
    B, T, _ = q.shape
    blk = min(blk, T)
    bias = _prompt_bias_tiles(rel_bias, blk)
    far = rel_bias[NUM_BUCKETS - 1].astype(F32)
    vrows = vt16.shape[2]
    return pl.pallas_call(
        functools.partial(_attn_prompt_body, blk=blk, hp=hp, lam_init=lam_init),
        grid=(B, H_B // hp, T // blk),
        in_specs=[pl.BlockSpec(memory_space=pltpu.SMEM),
                  pl.BlockSpec((1, blk, hp * 2 * DK_B), lambda b, h, i: (b, i, h)),
                  pl.BlockSpec((1, T, hp * 2 * DK_B), lambda b, h, i: (b, 0, h)),
                  pl.BlockSpec((1, hp, vrows, T), lambda b, h, i: (b, h, 0, 0)),
                  pl.BlockSpec((hp, 2, blk, blk), lambda b, h, i: (h, 0, 0, 0)),
                  pl.BlockSpec((4, DK_B), lambda b, h, i: (0, 0)),
                  pl.BlockSpec((1, DV_B), lambda b, h, i: (0, 0))],
        out_specs=pl.BlockSpec((1, blk, hp * DV_B), lambda b, h, i: (b, i, h)),
        scratch_shapes=[pltpu.VMEM((hp, 2 * DK_B, 2 * blk), BF16), pltpu.VMEM((hp, 1, 2 * blk), F32),
                        pltpu.VMEM((hp, vrows, 2 * blk), F32)],
        out_shape=jax.ShapeDtypeStruct((B, T, H_B * DV_B), BF16),
        compiler_params=_params(("parallel", "parallel", "arbitrary")),
        name="attn_prompt",
    )(far, q, k16, vt16, bias, lam_q.astype(F32), subln_w.reshape(1, DV_B).astype(F32))


def _attn_sample_body(*refs, pp, t_new, lam_init):
    pt_ref, q_ref = refs[:2]
    k_refs = refs[2:2 + pp]
    v_refs = refs[2 + pp:2 + 2 * pp]
    kn_ref, vn_ref, bias_ref, lam_ref, sw_ref, o_ref, m_scr, l_scr, acc_scr = refs[2 + 2 * pp:]
    step = pl.program_id(1)
    last = step == pl.num_programs(1) - 1
    rows = 2 * t_new
    flat = PAGE_SIZE * H_B

    @pl.when(step == 0)
    def _():
        m_scr[...] = jnp.full(m_scr.shape, NEG_INF, F32)
        l_scr[...] = jnp.zeros(l_scr.shape, F32)
        acc_scr[...] = jnp.zeros(acc_scr.shape, F32)

    qall = jnp.concatenate([_stack_maps(q_ref[0, :, h * 2 * DK_B:(h + 1) * 2 * DK_B]) for h in range(H_B)],
                           axis=0).astype(BF16)

    def update(scores, values):
        m_old = m_scr[...]
        m_cur = jnp.max(scores[0], axis=1, keepdims=True)
        for s in scores[1:]:
            m_cur = jnp.maximum(m_cur, jnp.max(s, axis=1, keepdims=True))
        m_new = jnp.maximum(m_old, m_cur)
        alpha = jnp.exp2(m_old - m_new)
        l_new = alpha * l_scr[...]
        pv = None
        for s, v in zip(scores, values):
            p = jnp.exp2(s - m_new)
            l_new = l_new + jnp.sum(p, axis=1, keepdims=True)
            part = jnp.dot(p.astype(BF16), v, preferred_element_type=F32)
            pv = part if pv is None else pv + part
        l_scr[...] = l_new
        acc_scr[...] = alpha * acc_scr[...] + pv
        m_scr[...] = m_new

    def flat_page(ref):
        return ref[0].reshape(flat, 2 * DK_B).astype(BF16)

    scores = []
    for r in range(pp):
        bias = bias_ref[last.astype(jnp.int32)] if r == pp - 1 else bias_ref[0]
        scores.append(_dot_nt(qall, flat_page(k_refs[r])) + bias)
    update(scores, [flat_page(v_refs[r]) for r in range(pp)])

    @pl.when(last)
    def _():
        update([_dot_nt(qall, flat_page(kn_ref)) + bias_ref[2]], [flat_page(vn_ref)])
        w = acc_scr[...] / l_scr[...]
        lam = _lambda(lam_ref, lam_init)
        for h in range(H_B):
            o = w[h * rows:h * rows + t_new] - lam * w[h * rows + t_new:(h + 1) * rows]
            o = o * lax.rsqrt(jnp.mean(o * o, -1, keepdims=True) + 1e-5) * sw_ref[...] * (1.0 - lam_init)
            o_ref[0, :, h * DV_B:(h + 1) * DV_B] = o.astype(o_ref.dtype)


def _attn_sample(q, k_new, v_new, cache_k, cache_v, page_table, rel_bias, lam_q, subln_w, lam_init, *, pp=16):
    Bd, t_new, _ = q.shape
    n_pages = page_table.shape[1]
    assert n_pages % pp == 0
    bias = _sample_bias_tiles(rel_bias, t_new)
    padt = lambda a: jnp.pad(a, ((0, 0), (0, PAGE_SIZE - t_new), (0, 0), (0, 0)))
    page_spec = lambda r: pl.BlockSpec((1, PAGE_SIZE, H_B, 2 * DK_B),
                                       lambda b, s, pt, r=r: (pt[b, s * pp + r], 0, 0, 0))
    own_spec = pl.BlockSpec((1, PAGE_SIZE, H_B, 2 * DK_B), lambda b, s, pt: (b, 0, 0, 0))
    const = lambda *shape: pl.BlockSpec(shape, lambda b, s, pt: (0,) * len(shape))
    rows = H_B * 2 * t_new
    grid_spec = pltpu.PrefetchScalarGridSpec(
        num_scalar_prefetch=1,
        grid=(Bd, n_pages // pp),
        in_specs=([pl.BlockSpec((1, t_new, H_B * 2 * DK_B), lambda b, s, pt: (b, 0, 0))]
                  + [page_spec(r) for r in range(pp)] + [page_spec(r) for r in range(pp)]
                  + [own_spec, own_spec, const(3, rows, PAGE_SIZE * H_B), const(4, DK_B), const(1, DV_B)]),
        out_specs=pl.BlockSpec((1, t_new, H_B * DV_B), lambda b, s, pt: (b, 0, 0)),
        scratch_shapes=[pltpu.VMEM((rows, 1), F32), pltpu.VMEM((rows, 1), F32), pltpu.VMEM((rows, DV_B), F32)],
    )
    return pl.pallas_call(
        functools.partial(_attn_sample_body, pp=pp, t_new=t_new, lam_init=lam_init),
        grid_spec=grid_spec,
        out_shape=jax.ShapeDtypeStruct((Bd, t_new, H_B * DV_B), BF16),
        compiler_params=_params(("parallel", "arbitrary")),
        name="attn_sample",
    )(page_table, q, *([cache_k] * pp), *([cache_v] * pp), padt(k_new), padt(v_new), bias,
      lam_q.astype(F32), subln_w.reshape(1, DV_B).astype(F32))


def _gdn_mixer(x2d, B, T, w_qkv, w_zba, conv_w, a_log, dt_bias, gnorm_w, conv_state, ssm_state):
    state8 = jnp.pad(conv_state.astype(F32), ((0, 0), (8 - (CONV_W - 1), 0), (0, 0)))
    fused_tile = 512
    if T % fused_tile == 0:
        qkvc, tail8 = _inproj_conv(x2d, w_qkv, state8, conv_w.astype(F32), T, tm=fused_tile)
        qkvc = qkvc.reshape(B, T, CONV_DIM)
        xpad_tail = tail8[:, 8 - (CONV_W - 1):]
    else:
        proj = _mm(x2d, w_qkv).reshape(B, T, CONV_DIM)
        qkvc = _gdn_prep(proj, state8, conv_w.astype(F32), tt=fused_tile)
        xpad_tail = jnp.concatenate([conv_state.astype(F32), proj], axis=1)[:, -(CONV_W - 1):]
    zba2d = _mm(x2d, w_zba, tm=512)
    proj = zba2d.reshape(B, T, -1)
    bgc = _gdn_gates(zba2d, VDIM_A // LANES, a_log, dt_bias, min(CHUNK, T)).reshape(B, T, LANES)
    if T % CHUNK:
        tp = -(-T // CHUNK) * CHUNK
        padt = lambda a: jnp.pad(a, ((0, 0), (0, tp - T), (0, 0)))
        lane = jnp.arange(LANES)
        is_gc = (lane >= 2 * H_V) & (lane < 3 * H_V)
        tail = jnp.broadcast_to(jnp.where(is_gc, bgc[:, T - 1:T], 0.0), (B, tp - T, LANES))
        bgc = jnp.concatenate([bgc, tail], axis=1)
        qkvc = padt(qkvc)
        z_arr = padt(proj[:, :, :VDIM_A])
    else:
        z_arr = proj
    u, w, qg, kd, aqk, dec = _gdn_intra(qkvc, bgc)
    y, s_new = _gdn_state(u, w, qg, kd, aqk, dec, z_arr, 0, ssm_state, gnorm_w)
    return y[:, :T].reshape(B * T, VDIM_A), xpad_tail, s_new


def kernel(x_prompt, x_sample, state_conv, state_ssm, cache_k, cache_v, page_table, p_prompt, p_sample,
           w_in_a, conv_w_a, a_log_a, dt_bias_a, gnorm_w_a, w_out_a, w_kv, w_q_b, lam_b, subln_w_b, w_o_b,
           rel_bias, ln_g, ln_b, w_up, w_down, w_ple, w_pg):
    bf = lambda w: w.astype(BF16)
    w_qkv = bf(w_in_a[:, :, :CONV_DIM])
    w_zba = bf(jnp.pad(w_in_a[:, :, CONV_DIM:], ((0, 0), (0, 0), (0, IN_A_PAD - IN_A))))
    w_out, w_kv16, w_q, w_o = bf(w_out_a), bf(w_kv), bf(w_q_b), bf(w_o_b)
    w_up16, w_down16, w_ple16, w_pg16 = bf(w_up), bf(w_down), bf(w_ple), bf(w_pg)

    def run(x3d, p, conv_state, ssm_state, attn_fn):
        B, T, _ = x3d.shape
        x = x3d.reshape(B * T, D_MODEL)
        convs, ssms = [], []
        k_sh = v_sh = kv16 = None
        for i in range(DEPTH):
            if i < N_A:
                y, cs, ss = _gdn_mixer(x, B, T, w_qkv[i], w_zba[i], conv_w_a[i], a_log_a[i], dt_bias_a[i],
                                       gnorm_w_a[i], conv_state[i], ssm_state[i])
                convs.append(cs)
                ssms.append(ss)
                w_mix = w_out[i]
            else:
                j = i - N_A
                lam_init = 0.8 - 0.6 * math.exp(-0.3 * i)
                q = _mm(x, w_q[j]).reshape(B, T, H_B * 2 * DK_B)
                y = attn_fn(q, k_sh, v_sh, kv16, lam_b[j], subln_w_b[j], lam_init).reshape(B * T, H_B * DV_B)
                w_mix = w_o[j]
            x = _layer_tail(x, y, p[i].reshape(B * T, -1), w_mix, ln_g[i, 0], ln_b[i, 0], w_up16[i], w_down16[i],
                            ln_g[i, 1], ln_b[i, 1], w_ple16[i], w_pg16[i])
            if i == N_A - 1:
                k_sh, v_sh, k16, vt16 = _kv_proj(x, w_kv16, T)
                k_sh = k_sh.reshape(B, T, H_B, 2 * DK_B)
                v_sh = v_sh.reshape(B, T, H_B, DV_B)
                kv16 = (k16.reshape(B, T, -1), vt16)
        return x.reshape(B, T, D_MODEL), jnp.stack(convs), jnp.stack(ssms), k_sh, v_sh

    def attn_prompt(q, k_sh, v_sh, kv16, lam_q, sw, lam_init):
        return _attn_prompt(q, kv16[0], kv16[1], rel_bias, lam_q, sw, lam_init)

    def attn_sample(q, k_sh, v_sh, kv16, lam_q, sw, lam_init):
        return _attn_sample(q, k_sh, v_sh, cache_k, cache_v, page_table, rel_bias, lam_q, sw, lam_init)

    Bp = x_prompt.shape[0]
    conv0 = jnp.zeros((N_A, Bp, CONV_W - 1, CONV_DIM), x_prompt.dtype)
    ssm0 = jnp.zeros((N_A, Bp, H_V, DK_A, DV_A), state_ssm.dtype)
    y_prompt, conv_p, ssm_p, k_p, v_p = run(x_prompt, p_prompt, conv0, ssm0, attn_prompt)
    y_sample, conv_s, ssm_s, k_s, v_s = run(x_sample, p_sample, state_conv, state_ssm, attn_sample)
    return (y_prompt, y_sample, conv_p, ssm_p, k_p, v_p, conv_s, ssm_s, k_s, v_s)
```

```python
import functools
import math

import jax
import jax.numpy as jnp
from jax import lax
from jax.experimental import pallas as pl
from jax.experimental.pallas import tpu as pltpu

F32 = jnp.float32
BF16 = jnp.bfloat16

D_MODEL = 1024
DEPTH = 2
N_A = DEPTH // 2
PAGE_SIZE = 128
H_QK = 8
H_V = 16
DK_A = 128
DV_A = 128
CONV_W = 4
CHUNK = 64
QK_DIM = H_QK * DK_A
VDIM_A = H_V * DV_A
CONV_DIM = 2 * QK_DIM + VDIM_A
IN_A = CONV_DIM + VDIM_A + 2 * H_V
IN_A_PAD = CONV_DIM + VDIM_A + 512
H_B = 8
DK_B = 64
DV_B = 128
NUM_BUCKETS = 32
MAX_DISTANCE = 128
D_FF = 4 * D_MODEL
ALPHA = (2 * DEPTH) ** 0.25

LANES = 128
VMEM_LIMIT = 56 * 1024 * 1024
NEG_INF = float("-inf")
LOG2E = math.log2(math.e)


def _params(sem, vmem=VMEM_LIMIT):
    return pltpu.CompilerParams(dimension_semantics=sem, vmem_limit_bytes=vmem)


def _sigmoid(x):
    return 1.0 / (1.0 + jnp.exp(-x))


def _dot_nt(a, b):
    return lax.dot_general(a, b, (((1,), (1,)), ((), ())), preferred_element_type=F32)


def _dot_tn(a, b):
    return lax.dot_general(a, b, (((0,), (0,)), ((), ())), preferred_element_type=F32)


def _mm_body(x_ref, w_ref, *o_refs, act):
    acc = jnp.dot(x_ref[...].astype(BF16), w_ref[...], preferred_element_type=F32)
    if act == "relu2":
        r = jnp.maximum(acc, 0.0)
        acc = r * r
    for o_ref in o_refs:
        o_ref[...] = acc.astype(o_ref.dtype)


def _mm(x, w, *, act=None, out_dtypes=(F32,), tm=1024, tn=None):
    M, K = x.shape
    N = w.shape[1]
    tm = min(tm, M)
    tn = N if tn is None else tn
    assert M % tm == 0 and N % tn == 0
    outs = pl.pallas_call(
        functools.partial(_mm_body, act=act),
        grid=(M // tm, N // tn),
        in_specs=[pl.BlockSpec((tm, K), lambda i, j: (i, 0)),
                  pl.BlockSpec((K, tn), lambda i, j: (0, j))],
        out_specs=[pl.BlockSpec((tm, tn), lambda i, j: (i, j)) for _ in out_dtypes],
        out_shape=[jax.ShapeDtypeStruct((M, N), dt) for dt in out_dtypes],
        compiler_params=_params(("parallel", "arbitrary")),
        name="mm",
    )(x, w)
    return outs[0] if len(out_dtypes) == 1 else outs


V_ONES_ROWS = 16


def _kv_proj_body(x_ref, w_ref, k_ref, v_ref, k16_ref, vt16_ref):
    acc = jnp.dot(x_ref[...].astype(BF16), w_ref[...], preferred_element_type=F32)
    nk = H_B * 2 * DK_B
    k_ref[...] = pltpu.einshape("m(hd)->mhd", acc[:, :nk], h=H_B)
    v_ref[...] = pltpu.einshape("m(hd)->mhd", acc[:, nk:], h=H_B)
    k16_ref[...] = acc[:, :nk].astype(BF16)
    for h in range(H_B):
        vt16_ref[0, h, :DV_B, :] = acc[:, nk + h * DV_B:nk + (h + 1) * DV_B].T.astype(BF16)
        vt16_ref[0, h, DV_B:, :] = jnp.ones((V_ONES_ROWS, acc.shape[0]), BF16)


def _kv_proj(x, w_kv, T, *, tm=512):
    M, K = x.shape
    N = w_kv.shape[1]
    tm = min(tm, T)
    assert T % tm == 0
    tps = T // tm
    nk = H_B * 2 * DK_B
    return pl.pallas_call(
        _kv_proj_body,
        grid=(M // tm,),
        in_specs=[pl.BlockSpec((tm, K), lambda i: (i, 0)),
                  pl.BlockSpec((K, N), lambda i: (0, 0), pipeline_mode=pl.Buffered(1))],
        out_specs=[pl.BlockSpec((tm, H_B, 2 * DK_B), lambda i: (i, 0, 0)),
                   pl.BlockSpec((tm, H_B, DV_B), lambda i: (i, 0, 0)),
                   pl.BlockSpec((tm, nk), lambda i: (i, 0)),
                   pl.BlockSpec((1, H_B, DV_B + V_ONES_ROWS, tm), lambda i: (i // tps, 0, 0, i % tps))],
        out_shape=[jax.ShapeDtypeStruct((M, H_B, 2 * DK_B), F32), jax.ShapeDtypeStruct((M, H_B, DV_B), F32),
                   jax.ShapeDtypeStruct((M, nk), BF16),
                   jax.ShapeDtypeStruct((M // T, H_B, DV_B + V_ONES_ROWS, T), BF16)],
        compiler_params=_params(("parallel",)),
        name="kv_proj",
    )(x, w_kv)


def _layernorm_rows(y, g, b):
    mu = jnp.mean(y, -1, keepdims=True)
    yc = y - mu
    var = jnp.mean(yc * yc, -1, keepdims=True)
    return yc * lax.rsqrt(var + 1e-5) * g + b


def _layer_tail_body(x_ref, y_ref, p_ref, wo_ref, g1_ref, b1_ref, wup_ref, wdown_ref, g2_ref, b2_ref,
                     wple_ref, wpg_ref, o_ref):
    x1 = _layernorm_rows(ALPHA * x_ref[...] + jnp.dot(y_ref[...], wo_ref[...], preferred_element_type=F32),
                         g1_ref[...], b1_ref[...])
    u = jnp.maximum(jnp.dot(x1.astype(BF16), wup_ref[...], preferred_element_type=F32), 0.0)
    h = (u * u).astype(BF16)
    x2 = _layernorm_rows(ALPHA * x1 + jnp.dot(h, wdown_ref[...], preferred_element_type=F32),
                         g2_ref[...], b2_ref[...])
    e = jnp.dot(p_ref[...].astype(BF16), wple_ref[...], preferred_element_type=F32)
    gt = jnp.dot(x2.astype(BF16), wpg_ref[...], preferred_element_type=F32)
    o_ref[...] = x2 + e * _sigmoid(gt)


def _layer_tail(x, y, p, w_o, g1, b1, w_up, w_down, g2, b2, w_ple, w_pg, *, tm=512):
    M, D = x.shape
    Ky = y.shape[1]
    P = p.shape[1]
    F = w_up.shape[1]
    tm = min(tm, M)
    resident = lambda r, c: pl.BlockSpec((r, c), lambda i: (0, 0), pipeline_mode=pl.Buffered(1))
    row = lambda v: v.reshape(1, D)
    return pl.pallas_call(
        _layer_tail_body,
        grid=(M // tm,),
        in_specs=[pl.BlockSpec((tm, D), lambda i: (i, 0)),
                  pl.BlockSpec((tm, Ky), lambda i: (i, 0)),
                  pl.BlockSpec((tm, P), lambda i: (i, 0)),
                  resident(Ky, D), resident(1, D), resident(1, D),
                  resident(D, F), resident(F, D), resident(1, D), resident(1, D), resident(P, D), resident(D, D)],
        out_specs=pl.BlockSpec((tm, D), lambda i: (i, 0)),
        out_shape=jax.ShapeDtypeStruct((M, D), F32),
        compiler_params=_params(("parallel",)),
        name="layer_tail",
    )(x, y, p, w_o, row(g1), row(b1), w_up, w_down, row(g2), row(b2), w_ple, w_pg)


def _conv_silu_norm(xf_ref, cw_ref, store, first_group):
    tt = xf_ref.shape[0] - 8
    for h in range(xf_ref.shape[1] // LANES):
        sl = slice(h * LANES, (h + 1) * LANES)
        cw = cw_ref[:, sl]
        c = xf_ref[8:, sl] * cw[3:4]
        for s in (1, 2, 3):
            c = c + xf_ref[8 - s:8 - s + tt, sl] * cw[3 - s:4 - s]
        c = c * _sigmoid(c)
        group = first_group + h
        nrm = lax.rsqrt(jnp.sum(c * c, -1, keepdims=True) + 1e-6)
        scale = jnp.where(group < H_QK, nrm * (DK_A ** -0.5), jnp.where(group < 2 * H_QK, nrm, 1.0))
        store(sl, c * scale)


def _gdn_prep_body(x_ref, halo_ref, st_ref, cw_ref, o_ref, xf_scr, *, gw, nb):
    t = pl.program_id(1)
    j = pl.program_id(2)
    for bi in range(nb):
        xf_scr[:8] = jnp.where(t == 0, st_ref[bi], halo_ref[bi])
        xf_scr[8:] = x_ref[bi]

        def store(sl, val, bi=bi):
            o_ref[bi, :, sl] = val

        _conv_silu_norm(xf_scr, cw_ref, store, j * (gw // LANES))


def _inproj_conv_body(x_ref, w_ref, wz_ref, st_ref, cw_ref, o_ref, zo_ref, tail_ref, carry_scr, xf_scr,
                      *, tiles_per_seq):
    i = pl.program_id(0)
    j = pl.program_id(1)
    tn = w_ref.shape[1]
    x16 = x_ref[...].astype(BF16)
    xf_scr[:8] = jnp.where(i % tiles_per_seq == 0, st_ref[0], carry_scr[j])
    xf_scr[8:] = jnp.dot(x16, w_ref[...], preferred_element_type=F32)
    zo_ref[...] = jnp.dot(x16, wz_ref[...], preferred_element_type=F32)

    def store(sl, val):
        o_ref[:, sl] = val

    _conv_silu_norm(xf_scr, cw_ref, store, j * (tn // LANES))
    last8 = xf_scr[xf_scr.shape[0] - 8:]
    carry_scr[j] = last8
    tail_ref[0] = last8


def _inproj_conv(x2d, w_qkv, w_zba, state8, conv_w, T, *, tm=512, tn=1024):
    M, K = x2d.shape
    nj = CONV_DIM // tn
    nz = w_zba.shape[1]
    assert T % tm == 0 and CONV_DIM % tn == 0 and nz % (nj * LANES) == 0
    tz = nz // nj
    tps = T // tm
    qkvc, zba, tails = pl.pallas_call(
        functools.partial(_inproj_conv_body, tiles_per_seq=tps),
        grid=(M // tm, nj),
        in_specs=[pl.BlockSpec((tm, K), lambda i, j: (i, 0)),
                  pl.BlockSpec((K, tn), lambda i, j: (0, j)),
                  pl.BlockSpec((K, tz), lambda i, j: (0, j)),
                  pl.BlockSpec((1, 8, tn), lambda i, j: (i // tps, 0, j)),
                  pl.BlockSpec((CONV_W, tn), lambda i, j: (0, j))],
        out_specs=[pl.BlockSpec((tm, tn), lambda i, j: (i, j)),
                   pl.BlockSpec((tm, tz), lambda i, j: (i, j)),
                   pl.BlockSpec((1, 8, tn), lambda i, j: (i, 0, j))],
        out_shape=[jax.ShapeDtypeStruct((M, CONV_DIM), F32), jax.ShapeDtypeStruct((M, nz), F32),
                   jax.ShapeDtypeStruct((M // tm, 8, CONV_DIM), F32)],
        scratch_shapes=[pltpu.VMEM((nj, 8, tn), F32), pltpu.VMEM((tm + 8, tn), F32)],
        compiler_params=_params(("arbitrary", "arbitrary")),
        name="inproj_conv",
    )(x2d, w_qkv, w_zba, state8, conv_w)
    return qkvc, zba, tails[tps - 1::tps]


def _gdn_prep(proj, state8, conv_w, *, tt, gw=512):
    B, T, _ = proj.shape
    tt = min(tt, T)
    hb = tt // 8
    nb = min(max(64 // tt, 1), B)
    assert B % nb == 0
    return pl.pallas_call(
        functools.partial(_gdn_prep_body, gw=gw, nb=nb),
        grid=(B // nb, T // tt, CONV_DIM // gw),
        in_specs=[pl.BlockSpec((nb, tt, gw), lambda b, t, j: (b, t, j)),
                  pl.BlockSpec((nb, 8, gw), lambda b, t, j: (b, jnp.maximum(t * hb - 1, 0), j)),
                  pl.BlockSpec((nb, 8, gw), lambda b, t, j: (b, 0, j)),
                  pl.BlockSpec((CONV_W, gw), lambda b, t, j: (0, j))],
        out_specs=pl.BlockSpec((nb, tt, gw), lambda b, t, j: (b, t, j)),
        out_shape=jax.ShapeDtypeStruct((B, T, CONV_DIM), F32),
        scratch_shapes=[pltpu.VMEM((tt + 8, gw), F32)],
        compiler_params=_params(("parallel", "parallel", "parallel")),
        name="gdn_prep",
    )(proj, proj, state8, conv_w)


def _gdn_gates_body(x_ref, alog_ref, dtb_ref, o_ref, *, period):
    ba = x_ref[...]
    lane = lax.broadcasted_iota(jnp.int32, ba.shape, 1)
    beta = _sigmoid(ba)
    xx = ba + dtb_ref[...]
    softplus = jnp.maximum(xx, 0.0) + jnp.log(1.0 + jnp.exp(-jnp.abs(xx)))
    g = -jnp.exp(alog_ref[...]) * softplus
    pos = lax.broadcasted_iota(jnp.int32, ba.shape, 0) % period
    gc = g
    s = 1
    while s < period:
        gc = gc + jnp.where(pos >= s, pltpu.roll(gc, s, 0), 0.0)
        s *= 2
    gc = pltpu.roll(gc, H_V, 1)
    o_ref[...] = jnp.where(lane < H_V, beta,
                           jnp.where(lane < 2 * H_V, g, jnp.where(lane < 3 * H_V, gc, 0.0)))


def _gdn_gates(proj2d, col, a_log, dt_bias, period, *, tm=1024):
    M = proj2d.shape[0]
    tm = min(tm, M)
    assert tm % period == 0
    pad = lambda v: jnp.pad(v.astype(F32), (H_V, LANES - 2 * H_V)).reshape(1, LANES)
    return pl.pallas_call(
        functools.partial(_gdn_gates_body, period=period),
        grid=(M // tm,),
        in_specs=[pl.BlockSpec((tm, LANES), lambda i: (i, col)),
                  pl.BlockSpec((1, LANES), lambda i: (0, 0)),
                  pl.BlockSpec((1, LANES), lambda i: (0, 0))],
        out_specs=pl.BlockSpec((tm, LANES), lambda i: (i, 0)),
        out_shape=jax.ShapeDtypeStruct((M, LANES), F32),
        compiler_params=_params(("parallel",)),
        name="gdn_gates",
    )(proj2d, pad(a_log), pad(dt_bias))


def _gdn_intra_body(q_ref, k_ref, v_ref, bgc_ref, gct_ref, u_ref, w_ref, qg_ref, kd_ref, aqk_ref, dec_ref,
                    *, nb, ca):
    hk = pl.program_id(2)
    C = CHUNK
    row = lax.broadcasted_iota(jnp.int32, (C, C), 0)
    col = lax.broadcasted_iota(jnp.int32, (C, C), 1)
    incl = row >= col
    strict = row > col
    eye = (row == col).astype(F32)
    lane = lax.broadcasted_iota(jnp.int32, (C, LANES), 1)
    blocks = [(bi, ci) for bi in range(nb) for ci in range(ca)]
    at = lambda x: (blocks[x][0], slice(blocks[x][1] * C, (blocks[x][1] + 1) * C))
    full = (slice(None),)

    def setup(x):
        bi, ci = blocks[x]
        k16 = k_ref[at(x) + full].astype(BF16)
        kk = _dot_nt(k16, k16)
        qk = _dot_nt(q_ref[at(x) + full].astype(BF16), k16)
        bgc = bgc_ref[at(x) + full]
        out = []
        for e in range(2):
            hv = 2 * hk + e
            b = jnp.sum(jnp.where(lane == hv, bgc, 0.0), axis=1, keepdims=True)
            g = jnp.sum(jnp.where(lane == 2 * H_V + hv, bgc, 0.0), axis=1, keepdims=True)
            dm = jnp.exp(jnp.where(incl, g - gct_ref[bi, ci, pl.ds(hv, 1), :], NEG_INF))
            aqk_ref[at(x) + (slice(e * C, (e + 1) * C),)] = jnp.where(incl, qk * dm, 0.0).astype(BF16)
            out.append((x, e, b, g, jnp.where(strict, -(b * kk * dm), 0.0)))
        return out

    def finish(chains, tinv):
        eg = [jnp.exp(g) for (_, _, _, g, _) in chains]
        rhs = []
        for (x, e, b, g, _), egy in zip(chains, eg):
            k = k_ref[at(x) + full]
            v = v_ref[at(x) + (slice(e * DV_A, (e + 1) * DV_A),)]
            rhs.append(jnp.concatenate([v * b, k * (b * egy)], axis=1).astype(BF16))
        uw = [jnp.dot(t.astype(BF16), r, preferred_element_type=F32) for t, r in zip(tinv, rhs)]
        for (x, e, b, g, _), egy, uwy in zip(chains, eg, uw):
            bi, ci = blocks[x]
            hs = at(x) + (slice(e * DV_A, (e + 1) * DV_A),)
            u_ref[hs] = uwy[:, :DV_A].astype(BF16)
            w_ref[hs] = uwy[:, DV_A:].astype(BF16)
            qg_ref[hs] = (q_ref[at(x) + full] * egy).astype(BF16)
            g_last = g[C - 1:C, :]
            kd_ref[hs] = (k_ref[at(x) + full] * jnp.exp(g_last - g)).astype(BF16)
            dec_ref[bi, 0, ci, e:e + 1, :] = jnp.broadcast_to(jnp.exp(g_last), (1, LANES))

    chains = [c for x in range(len(blocks)) for c in setup(x)]
    npow = [c[4] for c in chains]
    tinv = [eye + n for n in npow]
    for _ in range(5):
        n16 = [n.astype(BF16) for n in npow]
        npow = [jnp.dot(n, n, preferred_element_type=F32) for n in n16]
        tinv = [t + jnp.dot(t.astype(BF16), n.astype(BF16), preferred_element_type=F32)
                for t, n in zip(tinv, npow)]
    finish(chains, tinv)


def _gdn_intra(qkvc, bgc, *, chains=32):
    B, T, _ = qkvc.shape
    nc = T // CHUNK
    ca = min(chains // 2, nc)
    nb = min(max(chains // (2 * ca), 1), B)
    assert nc % ca == 0 and B % nb == 0
    ta = ca * CHUNK
    gct = bgc[..., 2 * H_V:3 * H_V].reshape(B, nc, CHUNK, H_V).transpose(0, 1, 3, 2)
    kb = QK_DIM // DK_A
    wide = lambda: pl.BlockSpec((nb, ta, 2 * DV_A), lambda b, t, h: (b, t, h))
    return pl.pallas_call(
        functools.partial(_gdn_intra_body, nb=nb, ca=ca),
        grid=(B // nb, T // ta, H_QK),
        in_specs=[pl.BlockSpec((nb, ta, DK_A), lambda b, t, h: (b, t, h)),
                  pl.BlockSpec((nb, ta, DK_A), lambda b, t, h: (b, t, kb + h)),
                  pl.BlockSpec((nb, ta, 2 * DV_A), lambda b, t, h: (b, t, kb + h)),
                  pl.BlockSpec((nb, ta, LANES), lambda b, t, h: (b, t, 0)),
                  pl.BlockSpec((nb, ca, H_V, CHUNK), lambda b, t, h: (b, t, 0, 0))],
        out_specs=[wide(), wide(), wide(), wide(),
                   pl.BlockSpec((nb, ta, 2 * CHUNK), lambda b, t, h: (b, t, h)),
                   pl.BlockSpec((nb, 1, ca, 2, LANES), lambda b, t, h: (b, h, t, 0, 0))],
        out_shape=[jax.ShapeDtypeStruct((B, T, VDIM_A), BF16)] * 4
        + [jax.ShapeDtypeStruct((B, T, H_V * CHUNK), BF16),
           jax.ShapeDtypeStruct((B, H_QK, nc, 2, LANES), F32)],
        compiler_params=_params(("parallel", "parallel", "arbitrary")),
        name="gdn_intra",
    )(qkvc, qkvc, qkvc, bgc, gct)


def _gdn_state_body(u_ref, w_ref, qg_ref, kd_ref, aqk_ref, dec_ref, z_ref, s0_ref, gw_ref, y_ref, st_ref, s_scr,
                    *, cb, hg):
    t = pl.program_id(2)
    C = CHUNK

    @pl.when(t == 0)
    def _():
        s_scr[...] = s0_ref[0]

    def chunk(ci, carry):
        rows = pl.ds(pl.multiple_of(ci * C, C), C)
        cols = [slice(hh * DV_A, (hh + 1) * DV_A) for hh in range(hg)]
        s = [s_scr[hh] for hh in range(hg)]
        r = [jnp.dot(jnp.concatenate([w_ref[0, rows, cs], qg_ref[0, rows, cs]], axis=0), s[hh].astype(BF16),
                     preferred_element_type=F32) for hh, cs in enumerate(cols)]
        v_new = [(u_ref[0, rows, cs].astype(F32) - r[hh][:C]).astype(BF16) for hh, cs in enumerate(cols)]
        s_upd = [_dot_tn(kd_ref[0, rows, cs], v_new[hh]) for hh, cs in enumerate(cols)]
        o_in = [jnp.dot(aqk_ref[0, rows, hh * C:(hh + 1) * C], v_new[hh], preferred_element_type=F32)
                for hh in range(hg)]
        for hh, cs in enumerate(cols):
            dec = dec_ref[0, hh // 2, pl.ds(ci, 1), hh % 2, :]
            s_scr[hh] = s[hh] * dec + s_upd[hh]
            o = r[hh][C:] + o_in[hh]
            z = z_ref[0, rows, cs]
            y = o * lax.rsqrt(jnp.mean(o * o, -1, keepdims=True) + 1e-6) * gw_ref[...] * (z * _sigmoid(z))
            y_ref[0, rows, cs] = y.astype(y_ref.dtype)
        return carry

    lax.fori_loop(0, cb, chunk, 0)

    @pl.when(t == pl.num_programs(2) - 1)
    def _():
        st_ref[0] = s_scr[...]


def _gdn_state(u, w, qg, kd, aqk, dec, z_arr, z_off, s0, gnorm_w, *, cb=8, hg=16):
    B, T, _ = u.shape
    nc = T // CHUNK
    cb = min(cb, nc)
    tb = cb * CHUNK
    ng = H_V // hg
    assert z_off % (hg * DV_A) == 0
    z_col = z_off // (hg * DV_A)
    wide = lambda: pl.BlockSpec((1, tb, hg * DV_A), lambda b, g, t: (b, t, g))
    return pl.pallas_call(
        functools.partial(_gdn_state_body, cb=cb, hg=hg),
        grid=(B, ng, T // tb),
        in_specs=[wide(), wide(), wide(), wide(),
                  pl.BlockSpec((1, tb, hg * CHUNK), lambda b, g, t: (b, t, g)),
                  pl.BlockSpec((1, hg // 2, cb, 2, LANES), lambda b, g, t: (b, g, t, 0, 0)),
                  pl.BlockSpec((1, tb, hg * DV_A), lambda b, g, t: (b, t, z_col + g)),
                  pl.BlockSpec((1, hg, DK_A, DV_A), lambda b, g, t: (b, g, 0, 0)),
                  pl.BlockSpec((1, DV_A), lambda b, g, t: (0, 0))],
        out_specs=[wide(), pl.BlockSpec((1, hg, DK_A, DV_A), lambda b, g, t: (b, g, 0, 0))],
        out_shape=[jax.ShapeDtypeStruct((B, T, VDIM_A), BF16),
                   jax.ShapeDtypeStruct((B, H_V, DK_A, DV_A), F32)],
        scratch_shapes=[pltpu.VMEM((hg, DK_A, DV_A), F32)],
        compiler_params=_params(("parallel", "parallel", "arbitrary")),
        name="gdn_state",
    )(u, w, qg, kd, aqk, dec, z_arr, s0, gnorm_w.reshape(1, DV_A).astype(F32))


def _bucket_starts():
    max_exact = NUM_BUCKETS // 2
    starts = list(range(max_exact + 1))
    n = max_exact
    for b in range(max_exact + 1, NUM_BUCKETS):
        while max_exact + int(math.log(n / max_exact) / math.log(MAX_DISTANCE / max_exact)
                              * (NUM_BUCKETS - max_exact)) < b:
            n += 1
        starts.append(n)
    return starts


_BUCKET_START = _bucket_starts()


def _bias_of_distance(n, tab_ref, h):
    val = jnp.full(n.shape, tab_ref[(NUM_BUCKETS - 1) * H_B + h], F32)
    for b in range(NUM_BUCKETS - 2, -1, -1):
        val = jnp.where(n < _BUCKET_START[b + 1], tab_ref[b * H_B + h], val)
    return val


def _prompt_bias_body(tab_ref, o_ref, *, blk):
    h = pl.program_id(0)
    key = lax.broadcasted_iota(jnp.int32, (blk, blk), 0)
    qry = lax.broadcasted_iota(jnp.int32, (blk, blk), 1)
    d = qry - key
    o_ref[0, 0] = LOG2E * _bias_of_distance(d + blk, tab_ref, h)
    o_ref[0, 1] = jnp.where(d >= 0, LOG2E * _bias_of_distance(jnp.maximum(d, 0), tab_ref, h), NEG_INF)


def _prompt_bias_tiles(rel_bias, blk):
    assert blk >= _BUCKET_START[NUM_BUCKETS - 1]
    return pl.pallas_call(
        functools.partial(_prompt_bias_body, blk=blk),
        grid=(H_B,),
        in_specs=[pl.BlockSpec(memory_space=pltpu.SMEM)],
        out_specs=pl.BlockSpec((1, 2, blk, blk), lambda h: (h, 0, 0, 0)),
        out_shape=jax.ShapeDtypeStruct((H_B, 2, blk, blk), F32),
        compiler_params=_params(("arbitrary",)),
        name="prompt_bias",
    )(rel_bias.astype(F32).reshape(-1))


def _sample_bias_body(tab_ref, o_ref, *, t_new):
    h = pl.program_id(0)
    shape = (2 * t_new, PAGE_SIZE * H_B)
    r = lax.broadcasted_iota(jnp.int32, shape, 0)
    j = lax.broadcasted_iota(jnp.int32, shape, 1)
    tok = jnp.where(r >= t_new, r - t_new, r)
    c = j // H_B
    same_head = (j % H_B) == h
    far = jnp.full(shape, LOG2E * tab_ref[(NUM_BUCKETS - 1) * H_B + h], F32)
    o_ref[0] = jnp.where(same_head, far, NEG_INF)
    o_ref[1] = jnp.where(same_head, LOG2E * _bias_of_distance(PAGE_SIZE + tok - c, tab_ref, h), NEG_INF)
    d = tok - c
    o_ref[2] = jnp.where(same_head & (d >= 0), LOG2E * _bias_of_distance(jnp.maximum(d, 0), tab_ref, h), NEG_INF)


def _sample_bias_tiles(rel_bias, t_new):
    rows = 2 * t_new
    return pl.pallas_call(
        functools.partial(_sample_bias_body, t_new=t_new),
        grid=(H_B,),
        in_specs=[pl.BlockSpec(memory_space=pltpu.SMEM)],
        out_specs=pl.BlockSpec((3, rows, PAGE_SIZE * H_B), lambda h: (0, h, 0)),
        out_shape=jax.ShapeDtypeStruct((3, H_B * rows, PAGE_SIZE * H_B), F32),
        compiler_params=_params(("arbitrary",)),
        name="sample_bias",
    )(rel_bias.astype(F32).reshape(-1))


def _lambda(lam_ref, lam_init):
    lq = lam_ref[...]
    a = jnp.sum(lq[0:1] * lq[1:2], axis=1, keepdims=True)
    b = jnp.sum(lq[2:3] * lq[3:4], axis=1, keepdims=True)
    return jnp.exp(a) - jnp.exp(b) + lam_init


def _stack_maps(q):
    lane = lax.broadcasted_iota(jnp.int32, q.shape, 1)
    q = q * (DK_B ** -0.5 * LOG2E)
    return jnp.concatenate([jnp.where(lane < DK_B, q, 0.0), jnp.where(lane >= DK_B, q, 0.0)], axis=0)


def _attn_prompt_body(far_ref, q_ref, k_ref, vt_ref, bias_ref, lam_ref, sw_ref, o_ref,
                      qt_scr, m_scr, acc_scr, *, blk, hp, lam_init):
    h0 = pl.program_id(1) * hp
    i = pl.program_id(2)
    hd = 2 * DK_B
    for hh in range(hp):
        qt_scr[hh] = _stack_maps(q_ref[0, :, hh * hd:(hh + 1) * hd]).T.astype(BF16)
    m_scr[...] = jnp.full(m_scr.shape, NEG_INF, F32)
    acc_scr[...] = jnp.zeros(acc_scr.shape, F32)
    far = [far_ref[h0 + hh] * LOG2E for hh in range(hp)]
    gw = min(blk, 2 * LANES)
    groups = [(hh, slice(g * gw, (g + 1) * gw)) for hh in range(hp) for g in range(2 * blk // gw)]

    def bias_of(hh, cs, nkeys):
        c0 = cs.start % blk
        if nkeys == 2 * blk:
            return bias_ref[hh, :, :, c0:c0 + gw].reshape(2 * blk, gw)
        return bias_ref[hh, 1, :, c0:c0 + gw]

    def tile(start, nkeys, biased):
        kj = [k_ref[0, pl.ds(start, nkeys), hh * hd:(hh + 1) * hd] for hh in range(hp)]
        vtj = [vt_ref[0, hh, :, pl.ds(start, nkeys)] for hh in range(hp)]
        s = [jnp.dot(kj[hh], qt_scr[hh, :, cs], preferred_element_type=F32) for hh, cs in groups]
        if biased:
            s = [sg + bias_of(hh, cs, nkeys) for sg, (hh, cs) in zip(s, groups)]
        for sg, (hh, cs) in zip(s, groups):
            m_old = m_scr[hh, :, cs]
            m_cur = jnp.max(sg, axis=0, keepdims=True)
            if not biased:
                m_cur = m_cur + far[hh]
            mn = jnp.maximum(m_old, m_cur)
            p = jnp.exp2(sg - (mn if biased else mn - far[hh])).astype(BF16)
            pv = jnp.dot(vtj[hh], p, preferred_element_type=F32)
            acc_scr[hh, :, cs] = jnp.exp2(m_old - mn) * acc_scr[hh, :, cs] + pv
            m_scr[hh, :, cs] = mn

    n_far = jnp.maximum(i - 1, 0)

    def far_pair(jj, carry):
        tile(pl.multiple_of(jj * 2 * blk, 2 * blk), 2 * blk, False)
        return carry

    lax.fori_loop(0, n_far // 2, far_pair, 0)

    @pl.when(n_far % 2 == 1)
    def _():
        tile(pl.multiple_of((n_far - 1) * blk, blk), blk, False)

    @pl.when(i >= 1)
    def _():
        tile(pl.multiple_of((i - 1) * blk, blk), 2 * blk, True)

    @pl.when(i == 0)
    def _():
        tile(0, blk, True)

    lam = _lambda(lam_ref, lam_init)
    for hh in range(hp):
        acc = acc_scr[hh]
        wt = acc[:DV_B] / acc[DV_B:DV_B + 1]
        ot = wt[:, :blk] - lam * wt[:, blk:]
        ot = ot * lax.rsqrt(jnp.mean(ot * ot, axis=0, keepdims=True) + 1e-5)
        o_ref[0, :, hh * DV_B:(hh + 1) * DV_B] = (ot.T * sw_ref[...] * (1.0 - lam_init)).astype(o_ref.dtype)


def _attn_prompt(q, k16, vt16, rel_bias, lam_q, subln_w, lam_init, *, blk=512, hp=2):
    B, T, _ = q.shape
    blk = min(blk, T)
    bias = _prompt_bias_tiles(rel_bias, blk)
    far = rel_bias[NUM_BUCKETS - 1].astype(F32)
    vrows = vt16.shape[2]
    return pl.pallas_call(
        functools.partial(_attn_prompt_body, blk=blk, hp=hp, lam_init=lam_init),
        grid=(B, H_B // hp, T // blk),
        in_specs=[pl.BlockSpec(memory_space=pltpu.SMEM),
                  pl.BlockSpec((1, blk, hp * 2 * DK_B), lambda b, h, i: (b, i, h)),
                  pl.BlockSpec((1, T, hp * 2 * DK_B), lambda b, h, i: (b, 0, h)),
                  pl.BlockSpec((1, hp, vrows, T), lambda b, h, i: (b, h, 0, 0)),
                  pl.BlockSpec((hp, 2, blk, blk), lambda b, h, i: (h, 0, 0, 0)),
                  pl.BlockSpec((4, DK_B), lambda b, h, i: (0, 0)),
                  pl.BlockSpec((1, DV_B), lambda b, h, i: (0, 0))],
        out_specs=pl.BlockSpec((1, blk, hp * DV_B), lambda b, h, i: (b, i, h)),
        scratch_shapes=[pltpu.VMEM((hp, 2 * DK_B, 2 * blk), BF16), pltpu.VMEM((hp, 1, 2 * blk), F32),
                        pltpu.VMEM((hp, vrows, 2 * blk), F32)],
        out_shape=jax.ShapeDtypeStruct((B, T, H_B * DV_B), BF16),
        compiler_params=_params(("parallel", "parallel", "arbitrary")),
        name="attn_prompt",
    )(far, q, k16, vt16, bias, lam_q.astype(F32), subln_w.reshape(1, DV_B).astype(F32))


def _attn_sample_body(*refs, pp, t_new, lam_init):
    pt_ref, q_ref = refs[:2]
    k_refs = refs[2:2 + pp]
    v_refs = refs[2 + pp:2 + 2 * pp]
    kn_ref, vn_ref, bias_ref, lam_ref, sw_ref, o_ref, m_scr, l_scr, acc_scr = refs[2 + 2 * pp:]
    step = pl.program_id(1)
    last = step == pl.num_programs(1) - 1
    rows = 2 * t_new
    flat = PAGE_SIZE * H_B

    @pl.when(step == 0)
    def _():
        m_scr[...] = jnp.full(m_scr.shape, NEG_INF, F32)
        l_scr[...] = jnp.zeros(l_scr.shape, F32)
        acc_scr[...] = jnp.zeros(acc_scr.shape, F32)

    qall = jnp.concatenate([_stack_maps(q_ref[0, :, h * 2 * DK_B:(h + 1) * 2 * DK_B]) for h in range(H_B)],
                           axis=0).astype(BF16)

    def update(scores, values):
        m_old = m_scr[...]
        m_cur = jnp.max(scores[0], axis=1, keepdims=True)
        for s in scores[1:]:
            m_cur = jnp.maximum(m_cur, jnp.max(s, axis=1, keepdims=True))
        m_new = jnp.maximum(m_old, m_cur)
        alpha = jnp.exp2(m_old - m_new)
        l_new = alpha * l_scr[...]
        pv = None
        for s, v in zip(scores, values):
            p = jnp.exp2(s - m_new)
            l_new = l_new + jnp.sum(p, axis=1, keepdims=True)
            part = jnp.dot(p.astype(BF16), v, preferred_element_type=F32)
            pv = part if pv is None else pv + part
        l_scr[...] = l_new
        acc_scr[...] = alpha * acc_scr[...] + pv
        m_scr[...] = m_new

    def flat_page(ref):
        return ref[0].reshape(flat, 2 * DK_B).astype(BF16)

    scores = []
    for r in range(pp):
        bias = bias_ref[last.astype(jnp.int32)] if r == pp - 1 else bias_ref[0]
        scores.append(_dot_nt(qall, flat_page(k_refs[r])) + bias)
    update(scores, [flat_page(v_refs[r]) for r in range(pp)])

    @pl.when(last)
    def _():
        update([_dot_nt(qall, flat_page(kn_ref)) + bias_ref[2]], [flat_page(vn_ref)])
        w = acc_scr[...] / l_scr[...]
        lam = _lambda(lam_ref, lam_init)
        for h in range(H_B):
            o = w[h * rows:h * rows + t_new] - lam * w[h * rows + t_new:(h + 1) * rows]
            o = o * lax.rsqrt(jnp.mean(o * o, -1, keepdims=True) + 1e-5) * sw_ref[...] * (1.0 - lam_init)
            o_ref[0, :, h * DV_B:(h + 1) * DV_B] = o.astype(o_ref.dtype)


def _attn_sample(q, k_new, v_new, cache_k, cache_v, page_table, rel_bias, lam_q, subln_w, lam_init, *, pp=16):
    Bd, t_new, _ = q.shape
    n_pages = page_table.shape[1]
    assert n_pages % pp == 0
    bias = _sample_bias_tiles(rel_bias, t_new)
    padt = lambda a: jnp.pad(a, ((0, 0), (0, PAGE_SIZE - t_new), (0, 0), (0, 0)))
    page_spec = lambda r: pl.BlockSpec((1, PAGE_SIZE, H_B, 2 * DK_B),
                                       lambda b, s, pt, r=r: (pt[b, s * pp + r], 0, 0, 0))
    own_spec = pl.BlockSpec((1, PAGE_SIZE, H_B, 2 * DK_B), lambda b, s, pt: (b, 0, 0, 0))
    const = lambda *shape: pl.BlockSpec(shape, lambda b, s, pt: (0,) * len(shape))
    rows = H_B * 2 * t_new
    grid_spec = pltpu.PrefetchScalarGridSpec(
        num_scalar_prefetch=1,
        grid=(Bd, n_pages // pp),
        in_specs=([pl.BlockSpec((1, t_new, H_B * 2 * DK_B), lambda b, s, pt: (b, 0, 0))]
                  + [page_spec(r) for r in range(pp)] + [page_spec(r) for r in range(pp)]
                  + [own_spec, own_spec, const(3, rows, PAGE_SIZE * H_B), const(4, DK_B), const(1, DV_B)]),
        out_specs=pl.BlockSpec((1, t_new, H_B * DV_B), lambda b, s, pt: (b, 0, 0)),
        scratch_shapes=[pltpu.VMEM((rows, 1), F32), pltpu.VMEM((rows, 1), F32), pltpu.VMEM((rows, DV_B), F32)],
    )
    return pl.pallas_call(
        functools.partial(_attn_sample_body, pp=pp, t_new=t_new, lam_init=lam_init),
        grid_spec=grid_spec,
        out_shape=jax.ShapeDtypeStruct((Bd, t_new, H_B * DV_B), BF16),
        compiler_params=_params(("parallel", "arbitrary")),
        name="attn_sample",
    )(page_table, q, *([cache_k] * pp), *([cache_v] * pp), padt(k_new), padt(v_new), bias,
      lam_q.astype(F32), subln_w.reshape(1, DV_B).astype(F32))


def _gdn_mixer(x2d, B, T, w_qkv, w_zba, conv_w, a_log, dt_bias, gnorm_w, conv_state, ssm_state):
    state8 = jnp.pad(conv_state.astype(F32), ((0, 0), (8 - (CONV_W - 1), 0), (0, 0)))
    fused_tile = 512
    if T % fused_tile == 0:
        qkvc, zba2d, tail8 = _inproj_conv(x2d, w_qkv, w_zba, state8, conv_w.astype(F32), T, tm=fused_tile)
        qkvc = qkvc.reshape(B, T, CONV_DIM)
        xpad_tail = tail8[:, 8 - (CONV_W - 1):]
    else:
        proj = _mm(x2d, w_qkv).reshape(B, T, CONV_DIM)
        qkvc = _gdn_prep(proj, state8, conv_w.astype(F32), tt=fused_tile)
        xpad_tail = jnp.concatenate([conv_state.astype(F32), proj], axis=1)[:, -(CONV_W - 1):]
        zba2d = _mm(x2d, w_zba, tm=512)
    proj = zba2d.reshape(B, T, -1)
    bgc = _gdn_gates(zba2d, VDIM_A // LANES, a_log, dt_bias, min(CHUNK, T)).reshape(B, T, LANES)
    if T % CHUNK:
        tp = -(-T // CHUNK) * CHUNK
        padt = lambda a: jnp.pad(a, ((0, 0), (0, tp - T), (0, 0)))
        lane = jnp.arange(LANES)
        is_gc = (lane >= 2 * H_V) & (lane < 3 * H_V)
        tail = jnp.broadcast_to(jnp.where(is_gc, bgc[:, T - 1:T], 0.0), (B, tp - T, LANES))
        bgc = jnp.concatenate([bgc, tail], axis=1)
        qkvc = padt(qkvc)
        z_arr = padt(proj[:, :, :VDIM_A])
    else:
        z_arr = proj
    u, w, qg, kd, aqk, dec = _gdn_intra(qkvc, bgc)
    y, s_new = _gdn_state(u, w, qg, kd, aqk, dec, z_arr, 0, ssm_state, gnorm_w)
    return y[:, :T].reshape(B * T, VDIM_A), xpad_tail, s_new


def kernel(x_prompt, x_sample, state_conv, state_ssm, cache_k, cache_v, page_table, p_prompt, p_sample,
           w_in_a, conv_w_a, a_log_a, dt_bias_a, gnorm_w_a, w_out_a, w_kv, w_q_b, lam_b, subln_w_b, w_o_b,
           rel_bias, ln_g, ln_b, w_up, w_down, w_ple, w_pg):
    bf = lambda w: w.astype(BF16)
    w_qkv = bf(w_in_a[:, :, :CONV_DIM])
    w_zba = bf(jnp.pad(w_in_a[:, :, CONV_DIM:], ((0, 0), (0, 0), (0, IN_A_PAD - IN_A))))
    w_out, w_kv16, w_q, w_o = bf(w_out_a), bf(w_kv), bf(w_q_b), bf(w_o_b)
    w_up16, w_down16, w_ple16, w_pg16 = bf(w_up), bf(w_down), bf(w_ple), bf(w_pg)

    def run(x3d, p, conv_state, ssm_state, attn_fn):
        B, T, _ = x3d.shape
        x = x3d.reshape(B * T, D_MODEL)
        convs, ssms = [], []
        k_sh = v_sh = kv16 = None
        for i in range(DEPTH):
            if i < N_A:
                y, cs, ss = _gdn_mixer(x, B, T, w_qkv[i], w_zba[i], conv_w_a[i], a_log_a[i], dt_bias_a[i],
                                       gnorm_w_a[i], conv_state[i], ssm_state[i])
                convs.append(cs)
                ssms.append(ss)
                w_mix = w_out[i]
            else:
                j = i - N_A
                lam_init = 0.8 - 0.6 * math.exp(-0.3 * i)
                q = _mm(x, w_q[j]).reshape(B, T, H_B * 2 * DK_B)
                y = attn_fn(q, k_sh, v_sh, kv16, lam_b[j], subln_w_b[j], lam_init).reshape(B * T, H_B * DV_B)
                w_mix = w_o[j]
            x = _layer_tail(x, y, p[i].reshape(B * T, -1), w_mix, ln_g[i, 0], ln_b[i, 0], w_up16[i], w_down16[i],
                            ln_g[i, 1], ln_b[i, 1], w_ple16[i], w_pg16[i])
            if i == N_A - 1:
                k_sh, v_sh, k16, vt16 = _kv_proj(x, w_kv16, T)
                k_sh = k_sh.reshape(B, T, H_B, 2 * DK_B)
                v_sh = v_sh.reshape(B, T, H_B, DV_B)
                kv16 = (k16.reshape(B, T, -1), vt16)
        return x.reshape(B, T, D_MODEL), jnp.stack(convs), jnp.stack(ssms), k_sh, v_sh

    def attn_prompt(q, k_sh, v_sh, kv16, lam_q, sw, lam_init):
        return _attn_prompt(q, kv16[0], kv16[1], rel_bias, lam_q, sw, lam_init)

    def attn_sample(q, k_sh, v_sh, kv16, lam_q, sw, lam_init):
        return _attn_sample(q, k_sh, v_sh, cache_k, cache_v, page_table, rel_bias, lam_q, sw, lam_init)

    Bp = x_prompt.shape[0]
    conv0 = jnp.zeros((N_A, Bp, CONV_W - 1, CONV_DIM), x_prompt.dtype)
    ssm0 = jnp.zeros((N_A, Bp, H_V, DK_A, DV_A), state_ssm.dtype)
    y_prompt, conv_p, ssm_p, k_p, v_p = run(x_prompt, p_prompt, conv0, ssm0, attn_prompt)
    y_sample, conv_s, ssm_s, k_s, v_s = run(x_sample, p_sample, state_conv, state_ssm, attn_sample)
    return (y_prompt, y_sample, conv_p, ssm_p, k_p, v_p, conv_s, ssm_s, k_s, v_s)
```

```python
import functools
import math

import jax
import jax.numpy as jnp
from jax import lax
from jax.experimental import pallas as pl
from jax.experimental.pallas import tpu as pltpu

F32 = jnp.float32
BF16 = jnp.bfloat16

D_MODEL = 1024
DEPTH = 2
N_A = DEPTH // 2
PAGE_SIZE = 128
H_QK = 8
H_V = 16
DK_A = 128
DV_A = 128
CONV_W = 4
CHUNK = 64
QK_DIM = H_QK * DK_A
VDIM_A = H_V * DV_A
CONV_DIM = 2 * QK_DIM + VDIM_A
IN_A = CONV_DIM + VDIM_A + 2 * H_V
IN_A_PAD = CONV_DIM + VDIM_A + 512
H_B = 8
DK_B = 64
DV_B = 128
NUM_BUCKETS = 32
MAX_DISTANCE = 128
D_FF = 4 * D_MODEL
ALPHA = (2 * DEPTH) ** 0.25

LANES = 128
VMEM_LIMIT = 56 * 1024 * 1024
NEG_INF = float("-inf")
LOG2E = math.log2(math.e)


def _params(sem, vmem=VMEM_LIMIT):
    return pltpu.CompilerParams(dimension_semantics=sem, vmem_limit_bytes=vmem)


def _sigmoid(x):
    return 1.0 / (1.0 + jnp.exp(-x))


def _dot_nt(a, b):
    return lax.dot_general(a, b, (((1,), (1,)), ((), ())), preferred_element_type=F32)


def _dot_tn(a, b):
    return lax.dot_general(a, b, (((0,), (0,)), ((), ())), preferred_element_type=F32)


def _mm_body(x_ref, w_ref, *o_refs, act):
    acc = jnp.dot(x_ref[...].astype(BF16), w_ref[...], preferred_element_type=F32)
    if act == "relu2":
        r = jnp.maximum(acc, 0.0)
        acc = r * r
    for o_ref in o_refs:
        o_ref[...] = acc.astype(o_ref.dtype)


def _mm(x, w, *, act=None, out_dtypes=(F32,), tm=1024, tn=None):
    M, K = x.shape
    N = w.shape[1]
    tm = min(tm, M)
    tn = N if tn is None else tn
    assert M % tm == 0 and N % tn == 0
    outs = pl.pallas_call(
        functools.partial(_mm_body, act=act),
        grid=(M // tm, N // tn),
        in_specs=[pl.BlockSpec((tm, K), lambda i, j: (i, 0)),
                  pl.BlockSpec((K, tn), lambda i, j: (0, j))],
        out_specs=[pl.BlockSpec((tm, tn), lambda i, j: (i, j)) for _ in out_dtypes],
        out_shape=[jax.ShapeDtypeStruct((M, N), dt) for dt in out_dtypes],
        compiler_params=_params(("parallel", "arbitrary")),
        name="mm",
    )(x, w)
    return outs[0] if len(out_dtypes) == 1 else outs


V_ONES_ROWS = 16


def _kv_proj_body(x_ref, w_ref, k_ref, v_ref, k16_ref, vt16_ref):
    acc = jnp.dot(x_ref[...].astype(BF16), w_ref[...], preferred_element_type=F32)
    nk = H_B * 2 * DK_B
    k_ref[...] = pltpu.einshape("m(hd)->mhd", acc[:, :nk], h=H_B)
    v_ref[...] = pltpu.einshape("m(hd)->mhd", acc[:, nk:], h=H_B)
    k16_ref[...] = acc[:, :nk].astype(BF16)
    for h in range(H_B):
        vt16_ref[0, h, :DV_B, :] = acc[:, nk + h * DV_B:nk + (h + 1) * DV_B].T.astype(BF16)
        vt16_ref[0, h, DV_B:, :] = jnp.ones((V_ONES_ROWS, acc.shape[0]), BF16)


def _kv_proj(x, w_kv, T, *, tm=512):
    M, K = x.shape
    N = w_kv.shape[1]
    tm = min(tm, T)
    assert T % tm == 0
    tps = T // tm
    nk = H_B * 2 * DK_B
    return pl.pallas_call(
        _kv_proj_body,
        grid=(M // tm,),
        in_specs=[pl.BlockSpec((tm, K), lambda i: (i, 0)),
                  pl.BlockSpec((K, N), lambda i: (0, 0), pipeline_mode=pl.Buffered(1))],
        out_specs=[pl.BlockSpec((tm, H_B, 2 * DK_B), lambda i: (i, 0, 0)),
                   pl.BlockSpec((tm, H_B, DV_B), lambda i: (i, 0, 0)),
                   pl.BlockSpec((tm, nk), lambda i: (i, 0)),
                   pl.BlockSpec((1, H_B, DV_B + V_ONES_ROWS, tm), lambda i: (i // tps, 0, 0, i % tps))],
        out_shape=[jax.ShapeDtypeStruct((M, H_B, 2 * DK_B), F32), jax.ShapeDtypeStruct((M, H_B, DV_B), F32),
                   jax.ShapeDtypeStruct((M, nk), BF16),
                   jax.ShapeDtypeStruct((M // T, H_B, DV_B + V_ONES_ROWS, T), BF16)],
        compiler_params=_params(("parallel",)),
        name="kv_proj",
    )(x, w_kv)


def _layernorm_rows(y, g, b):
    mu = jnp.mean(y, -1, keepdims=True)
    yc = y - mu
    var = jnp.mean(yc * yc, -1, keepdims=True)
    return yc * lax.rsqrt(var + 1e-5) * g + b


def _layer_tail_body(x_ref, y_ref, p_ref, wo_ref, g1_ref, b1_ref, wup_ref, wdown_ref, g2_ref, b2_ref,
                     wple_ref, wpg_ref, o_ref):
    x1 = _layernorm_rows(ALPHA * x_ref[...] + jnp.dot(y_ref[...], wo_ref[...], preferred_element_type=F32),
                         g1_ref[...], b1_ref[...])
    u = jnp.maximum(jnp.dot(x1.astype(BF16), wup_ref[...], preferred_element_type=F32), 0.0)
    h = (u * u).astype(BF16)
    x2 = _layernorm_rows(ALPHA * x1 + jnp.dot(h, wdown_ref[...], preferred_element_type=F32),
                         g2_ref[...], b2_ref[...])
    e = jnp.dot(p_ref[...].astype(BF16), wple_ref[...], preferred_element_type=F32)
    gt = jnp.dot(x2.astype(BF16), wpg_ref[...], preferred_element_type=F32)
    o_ref[...] = x2 + e * _sigmoid(gt)


def _layer_tail(x, y, p, w_o, g1, b1, w_up, w_down, g2, b2, w_ple, w_pg, *, tm=512):
    M, D = x.shape
    Ky = y.shape[1]
    P = p.shape[1]
    F = w_up.shape[1]
    tm = min(tm, M)
    resident = lambda r, c: pl.BlockSpec((r, c), lambda i: (0, 0), pipeline_mode=pl.Buffered(1))
    row = lambda v: v.reshape(1, D)
    return pl.pallas_call(
        _layer_tail_body,
        grid=(M // tm,),
        in_specs=[pl.BlockSpec((tm, D), lambda i: (i, 0)),
                  pl.BlockSpec((tm, Ky), lambda i: (i, 0)),
                  pl.BlockSpec((tm, P), lambda i: (i, 0)),
                  resident(Ky, D), resident(1, D), resident(1, D),
                  resident(D, F), resident(F, D), resident(1, D), resident(1, D), resident(P, D), resident(D, D)],
        out_specs=pl.BlockSpec((tm, D), lambda i: (i, 0)),
        out_shape=jax.ShapeDtypeStruct((M, D), F32),
        compiler_params=_params(("parallel",)),
        name="layer_tail",
    )(x, y, p, w_o, row(g1), row(b1), w_up, w_down, row(g2), row(b2), w_ple, w_pg)


def _conv_silu_norm(xf_ref, cw_ref, store, first_group):
    tt = xf_ref.shape[0] - 8
    for h in range(xf_ref.shape[1] // LANES):
        sl = slice(h * LANES, (h + 1) * LANES)
        cw = cw_ref[:, sl]
        c = xf_ref[8:, sl] * cw[3:4]
        for s in (1, 2, 3):
            c = c + xf_ref[8 - s:8 - s + tt, sl] * cw[3 - s:4 - s]
        c = c * _sigmoid(c)
        group = first_group + h
        nrm = lax.rsqrt(jnp.sum(c * c, -1, keepdims=True) + 1e-6)
        scale = jnp.where(group < H_QK, nrm * (DK_A ** -0.5), jnp.where(group < 2 * H_QK, nrm, 1.0))
        store(sl, c * scale)


def _gdn_prep_body(x_ref, halo_ref, st_ref, cw_ref, o_ref, xf_scr, *, gw, nb):
    t = pl.program_id(1)
    j = pl.program_id(2)
    for bi in range(nb):
        xf_scr[:8] = jnp.where(t == 0, st_ref[bi], halo_ref[bi])
        xf_scr[8:] = x_ref[bi]

        def store(sl, val, bi=bi):
            o_ref[bi, :, sl] = val

        _conv_silu_norm(xf_scr, cw_ref, store, j * (gw // LANES))


def _inproj_conv_body(x_ref, w_ref, wz_ref, st_ref, cw_ref, o_ref, zo_ref, tail_ref, carry_scr, xf_scr,
                      *, tiles_per_seq):
    i = pl.program_id(0)
    j = pl.program_id(1)
    tn = w_ref.shape[1]
    x16 = x_ref[...].astype(BF16)
    xf_scr[:8] = jnp.where(i % tiles_per_seq == 0, st_ref[0], carry_scr[j])
    xf_scr[8:] = jnp.dot(x16, w_ref[...], preferred_element_type=F32)
    zo_ref[...] = jnp.dot(x16, wz_ref[...], preferred_element_type=F32)

    def store(sl, val):
        o_ref[:, sl] = val

    _conv_silu_norm(xf_scr, cw_ref, store, j * (tn // LANES))
    last8 = xf_scr[xf_scr.shape[0] - 8:]
    carry_scr[j] = last8
    tail_ref[0] = last8


def _inproj_conv(x2d, w_qkv, w_zba, state8, conv_w, T, *, tm=512, tn=1024):
    M, K = x2d.shape
    nj = CONV_DIM // tn
    nz = w_zba.shape[1]
    assert T % tm == 0 and CONV_DIM % tn == 0 and nz % (nj * LANES) == 0
    tz = nz // nj
    tps = T // tm
    qkvc, zba, tails = pl.pallas_call(
        functools.partial(_inproj_conv_body, tiles_per_seq=tps),
        grid=(M // tm, nj),
        in_specs=[pl.BlockSpec((tm, K), lambda i, j: (i, 0)),
                  pl.BlockSpec((K, tn), lambda i, j: (0, j)),
                  pl.BlockSpec((K, tz), lambda i, j: (0, j)),
                  pl.BlockSpec((1, 8, tn), lambda i, j: (i // tps, 0, j)),
                  pl.BlockSpec((CONV_W, tn), lambda i, j: (0, j))],
        out_specs=[pl.BlockSpec((tm, tn), lambda i, j: (i, j)),
                   pl.BlockSpec((tm, tz), lambda i, j: (i, j)),
                   pl.BlockSpec((1, 8, tn), lambda i, j: (i, 0, j))],
        out_shape=[jax.ShapeDtypeStruct((M, CONV_DIM), F32), jax.ShapeDtypeStruct((M, nz), F32),
                   jax.ShapeDtypeStruct((M // tm, 8, CONV_DIM), F32)],
        scratch_shapes=[pltpu.VMEM((nj, 8, tn), F32), pltpu.VMEM((tm + 8, tn), F32)],
        compiler_params=_params(("arbitrary", "arbitrary")),
        name="inproj_conv",
    )(x2d, w_qkv, w_zba, state8, conv_w)
    return qkvc, zba, tails[tps - 1::tps]


def _gdn_prep(proj, state8, conv_w, *, tt, gw=512):
    B, T, _ = proj.shape
    tt = min(tt, T)
    hb = tt // 8
    nb = min(max(64 // tt, 1), B)
    assert B % nb == 0
    return pl.pallas_call(
        functools.partial(_gdn_prep_body, gw=gw, nb=nb),
        grid=(B // nb, T // tt, CONV_DIM // gw),
        in_specs=[pl.BlockSpec((nb, tt, gw), lambda b, t, j: (b, t, j)),
                  pl.BlockSpec((nb, 8, gw), lambda b, t, j: (b, jnp.maximum(t * hb - 1, 0), j)),
                  pl.BlockSpec((nb, 8, gw), lambda b, t, j: (b, 0, j)),
                  pl.BlockSpec((CONV_W, gw), lambda b, t, j: (0, j))],
        out_specs=pl.BlockSpec((nb, tt, gw), lambda b, t, j: (b, t, j)),
        out_shape=jax.ShapeDtypeStruct((B, T, CONV_DIM), F32),
        scratch_shapes=[pltpu.VMEM((tt + 8, gw), F32)],
        compiler_params=_params(("parallel", "parallel", "parallel")),
        name="gdn_prep",
    )(proj, proj, state8, conv_w)


def _gdn_gates_body(x_ref, alog_ref, dtb_ref, o_ref, *, period):
    ba = x_ref[...]
    lane = lax.broadcasted_iota(jnp.int32, ba.shape, 1)
    beta = _sigmoid(ba)
    xx = ba + dtb_ref[...]
    softplus = jnp.maximum(xx, 0.0) + jnp.log(1.0 + jnp.exp(-jnp.abs(xx)))
    g = -jnp.exp(alog_ref[...]) * softplus
    pos = lax.broadcasted_iota(jnp.int32, ba.shape, 0) % period
    gc = g
    s = 1
    while s < period:
        gc = gc + jnp.where(pos >= s, pltpu.roll(gc, s, 0), 0.0)
        s *= 2
    gc = pltpu.roll(gc, H_V, 1)
    o_ref[...] = jnp.where(lane < H_V, beta,
                           jnp.where(lane < 2 * H_V, g, jnp.where(lane < 3 * H_V, gc, 0.0)))


def _gdn_gates(proj2d, col, a_log, dt_bias, period, *, tm=1024):
    M = proj2d.shape[0]
    tm = min(tm, M)
    assert tm % period == 0
    pad = lambda v: jnp.pad(v.astype(F32), (H_V, LANES - 2 * H_V)).reshape(1, LANES)
    return pl.pallas_call(
        functools.partial(_gdn_gates_body, period=period),
        grid=(M // tm,),
        in_specs=[pl.BlockSpec((tm, LANES), lambda i: (i, col)),
                  pl.BlockSpec((1, LANES), lambda i: (0, 0)),
                  pl.BlockSpec((1, LANES), lambda i: (0, 0))],
        out_specs=pl.BlockSpec((tm, LANES), lambda i: (i, 0)),
        out_shape=jax.ShapeDtypeStruct((M, LANES), F32),
        compiler_params=_params(("parallel",)),
        name="gdn_gates",
    )(proj2d, pad(a_log), pad(dt_bias))


def _gdn_intra_body(q_ref, k_ref, v_ref, bgc_ref, gct_ref, u_ref, w_ref, qg_ref, kd_ref, aqk_ref, dec_ref,
                    *, nb, ca):
    hk = pl.program_id(2)
    C = CHUNK
    row = lax.broadcasted_iota(jnp.int32, (C, C), 0)
    col = lax.broadcasted_iota(jnp.int32, (C, C), 1)
    incl = row >= col
    strict = row > col
    eye = (row == col).astype(F32)
    lane = lax.broadcasted_iota(jnp.int32, (C, LANES), 1)
    blocks = [(bi, ci) for bi in range(nb) for ci in range(ca)]
    at = lambda x: (blocks[x][0], slice(blocks[x][1] * C, (blocks[x][1] + 1) * C))
    full = (slice(None),)

    def setup(x):
        bi, ci = blocks[x]
        k16 = k_ref[at(x) + full].astype(BF16)
        kk = _dot_nt(k16, k16)
        qk = _dot_nt(q_ref[at(x) + full].astype(BF16), k16)
        bgc = bgc_ref[at(x) + full]
        out = []
        for e in range(2):
            hv = 2 * hk + e
            b = jnp.sum(jnp.where(lane == hv, bgc, 0.0), axis=1, keepdims=True)
            g = jnp.sum(jnp.where(lane == 2 * H_V + hv, bgc, 0.0), axis=1, keepdims=True)
            dm = jnp.exp(jnp.where(incl, g - gct_ref[bi, ci, pl.ds(hv, 1), :], NEG_INF))
            aqk_ref[at(x) + (slice(e * C, (e + 1) * C),)] = jnp.where(incl, qk * dm, 0.0).astype(BF16)
            out.append((x, e, b, g, jnp.where(strict, -(b * kk * dm), 0.0)))
        return out

    def finish(chains, tinv):
        eg = [jnp.exp(g) for (_, _, _, g, _) in chains]
        rhs = []
        for (x, e, b, g, _), egy in zip(chains, eg):
            k = k_ref[at(x) + full]
            v = v_ref[at(x) + (slice(e * DV_A, (e + 1) * DV_A),)]
            rhs.append(jnp.concatenate([v * b, k * (b * egy)], axis=1).astype(BF16))
        uw = [jnp.dot(t.astype(BF16), r, preferred_element_type=F32) for t, r in zip(tinv, rhs)]
        for (x, e, b, g, _), egy, uwy in zip(chains, eg, uw):
            bi, ci = blocks[x]
            hs = at(x) + (slice(e * DV_A, (e + 1) * DV_A),)
            u_ref[hs] = uwy[:, :DV_A].astype(BF16)
            w_ref[hs] = uwy[:, DV_A:].astype(BF16)
            qg_ref[hs] = (q_ref[at(x) + full] * egy).astype(BF16)
            g_last = g[C - 1:C, :]
            kd_ref[hs] = (k_ref[at(x) + full] * jnp.exp(g_last - g)).astype(BF16)
            dec_ref[bi, 0, ci, e:e + 1, :] = jnp.broadcast_to(jnp.exp(g_last), (1, LANES))

    chains = [c for x in range(len(blocks)) for c in setup(x)]
    npow = [c[4] for c in chains]
    tinv = [eye + n for n in npow]
    for _ in range(5):
        n16 = [n.astype(BF16) for n in npow]
        npow = [jnp.dot(n, n, preferred_element_type=F32) for n in n16]
        tinv = [t + jnp.dot(t.astype(BF16), n.astype(BF16), preferred_element_type=F32)
                for t, n in zip(tinv, npow)]
    finish(chains, tinv)


def _gdn_intra(qkvc, bgc, *, chains=32):
    B, T, _ = qkvc.shape
    nc = T // CHUNK
    ca = min(chains // 2, nc)
    nb = min(max(chains // (2 * ca), 1), B)
    assert nc % ca == 0 and B % nb == 0
    ta = ca * CHUNK
    gct = bgc[..., 2 * H_V:3 * H_V].reshape(B, nc, CHUNK, H_V).transpose(0, 1, 3, 2)
    kb = QK_DIM // DK_A
    wide = lambda: pl.BlockSpec((nb, ta, 2 * DV_A), lambda b, t, h: (b, t, h))
    return pl.pallas_call(
        functools.partial(_gdn_intra_body, nb=nb, ca=ca),
        grid=(B // nb, T // ta, H_QK),
        in_specs=[pl.BlockSpec((nb, ta, DK_A), lambda b, t, h: (b, t, h)),
                  pl.BlockSpec((nb, ta, DK_A), lambda b, t, h: (b, t, kb + h)),
                  pl.BlockSpec((nb, ta, 2 * DV_A), lambda b, t, h: (b, t, kb + h)),
                  pl.BlockSpec((nb, ta, LANES), lambda b, t, h: (b, t, 0)),
                  pl.BlockSpec((nb, ca, H_V, CHUNK), lambda b, t, h: (b, t, 0, 0))],
        out_specs=[wide(), wide(), wide(), wide(),
                   pl.BlockSpec((nb, ta, 2 * CHUNK), lambda b, t, h: (b, t, h)),
                   pl.BlockSpec((nb, 1, ca, 2, LANES), lambda b, t, h: (b, h, t, 0, 0))],
        out_shape=[jax.ShapeDtypeStruct((B, T, VDIM_A), BF16)] * 4
        + [jax.ShapeDtypeStruct((B, T, H_V * CHUNK), BF16),
           jax.ShapeDtypeStruct((B, H_QK, nc, 2, LANES), F32)],
        compiler_params=_params(("parallel", "parallel", "arbitrary")),
        name="gdn_intra",
    )(qkvc, qkvc, qkvc, bgc, gct)


def _gdn_state_body(u_ref, w_ref, qg_ref, kd_ref, aqk_ref, dec_ref, z_ref, s0_ref, gw_ref, y_ref, st_ref, s_scr,
                    *, cb, hg):
    t = pl.program_id(2)
    C = CHUNK

    @pl.when(t == 0)
    def _():
        s_scr[...] = s0_ref[0]

    def chunk(ci, carry):
        rows = pl.ds(pl.multiple_of(ci * C, C), C)
        cols = [slice(hh * DV_A, (hh + 1) * DV_A) for hh in range(hg)]
        s = [s_scr[hh] for hh in range(hg)]
        r = [jnp.dot(jnp.concatenate([w_ref[0, rows, cs], qg_ref[0, rows, cs]], axis=0), s[hh].astype(BF16),
                     preferred_element_type=F32) for hh, cs in enumerate(cols)]
        v_new = [(u_ref[0, rows, cs].astype(F32) - r[hh][:C]).astype(BF16) for hh, cs in enumerate(cols)]
        s_upd = [_dot_tn(kd_ref[0, rows, cs], v_new[hh]) for hh, cs in enumerate(cols)]
        o_in = [jnp.dot(aqk_ref[0, rows, hh * C:(hh + 1) * C], v_new[hh], preferred_element_type=F32)
                for hh in range(hg)]
        for hh, cs in enumerate(cols):
            dec = dec_ref[0, hh // 2, pl.ds(ci, 1), hh % 2, :]
            s_scr[hh] = s[hh] * dec + s_upd[hh]
            o = r[hh][C:] + o_in[hh]
            z = z_ref[0, rows, cs]
            y = o * lax.rsqrt(jnp.mean(o * o, -1, keepdims=True) + 1e-6) * gw_ref[...] * (z * _sigmoid(z))
            y_ref[0, rows, cs] = y.astype(y_ref.dtype)
        return carry

    lax.fori_loop(0, cb, chunk, 0)

    @pl.when(t == pl.num_programs(2) - 1)
    def _():
        st_ref[0] = s_scr[...]


def _gdn_state(u, w, qg, kd, aqk, dec, z_arr, z_off, s0, gnorm_w, *, cb=8, hg=16):
    B, T, _ = u.shape
    nc = T // CHUNK
    cb = min(cb, nc)
    tb = cb * CHUNK
    ng = H_V // hg
    assert z_off % (hg * DV_A) == 0
    z_col = z_off // (hg * DV_A)
    wide = lambda: pl.BlockSpec((1, tb, hg * DV_A), lambda b, g, t: (b, t, g))
    return pl.pallas_call(
        functools.partial(_gdn_state_body, cb=cb, hg=hg),
        grid=(B, ng, T // tb),
        in_specs=[wide(), wide(), wide(), wide(),
                  pl.BlockSpec((1, tb, hg * CHUNK), lambda b, g, t: (b, t, g)),
                  pl.BlockSpec((1, hg // 2, cb, 2, LANES), lambda b, g, t: (b, g, t, 0, 0)),
                  pl.BlockSpec((1, tb, hg * DV_A), lambda b, g, t: (b, t, z_col + g)),
                  pl.BlockSpec((1, hg, DK_A, DV_A), lambda b, g, t: (b, g, 0, 0)),
                  pl.BlockSpec((1, DV_A), lambda b, g, t: (0, 0))],
        out_specs=[wide(), pl.BlockSpec((1, hg, DK_A, DV_A), lambda b, g, t: (b, g, 0, 0))],
        out_shape=[jax.ShapeDtypeStruct((B, T, VDIM_A), BF16),
                   jax.ShapeDtypeStruct((B, H_V, DK_A, DV_A), F32)],
        scratch_shapes=[pltpu.VMEM((hg, DK_A, DV_A), F32)],
        compiler_params=_params(("parallel", "parallel", "arbitrary")),
        name="gdn_state",
    )(u, w, qg, kd, aqk, dec, z_arr, s0, gnorm_w.reshape(1, DV_A).astype(F32))


def _bucket_starts():
    max_exact = NUM_BUCKETS // 2
    starts = list(range(max_exact + 1))
    n = max_exact
    for b in range(max_exact + 1, NUM_BUCKETS):
        while max_exact + int(math.log(n / max_exact) / math.log(MAX_DISTANCE / max_exact)
                              * (NUM_BUCKETS - max_exact)) < b:
            n += 1
        starts.append(n)
    return starts


_BUCKET_START = _bucket_starts()


def _bias_of_distance(n, tab_ref, h):
    val = jnp.full(n.shape, tab_ref[(NUM_BUCKETS - 1) * H_B + h], F32)
    for b in range(NUM_BUCKETS - 2, -1, -1):
        val = jnp.where(n < _BUCKET_START[b + 1], tab_ref[b * H_B + h], val)
    return val


def _prompt_bias_body(tab_ref, o_ref, *, blk):
    h = pl.program_id(0)
    key = lax.broadcasted_iota(jnp.int32, (blk, blk), 0)
    qry = lax.broadcasted_iota(jnp.int32, (blk, blk), 1)
    d = qry - key
    o_ref[0, 0] = LOG2E * _bias_of_distance(d + blk, tab_ref, h)
    o_ref[0, 1] = jnp.where(d >= 0, LOG2E * _bias_of_distance(jnp.maximum(d, 0), tab_ref, h), NEG_INF)


def _prompt_bias_tiles(rel_bias, blk):
    assert blk >= _BUCKET_START[NUM_BUCKETS - 1]
    return pl.pallas_call(
        functools.partial(_prompt_bias_body, blk=blk),
        grid=(H_B,),
        in_specs=[pl.BlockSpec(memory_space=pltpu.SMEM)],
        out_specs=pl.BlockSpec((1, 2, blk, blk), lambda h: (h, 0, 0, 0)),
        out_shape=jax.ShapeDtypeStruct((H_B, 2, blk, blk), F32),
        compiler_params=_params(("arbitrary",)),
        name="prompt_bias",
    )(rel_bias.astype(F32).reshape(-1))


def _sample_bias_body(tab_ref, o_ref, *, t_new):
    h = pl.program_id(0)
    shape = (2 * t_new, PAGE_SIZE * H_B)
    r = lax.broadcasted_iota(jnp.int32, shape, 0)
    j = lax.broadcasted_iota(jnp.int32, shape, 1)
    tok = jnp.where(r >= t_new, r - t_new, r)
    c = j // H_B
    same_head = (j % H_B) == h
    far = jnp.full(shape, LOG2E * tab_ref[(NUM_BUCKETS - 1) * H_B + h], F32)
    o_ref[0] = jnp.where(same_head, far, NEG_INF)
    o_ref[1] = jnp.where(same_head, LOG2E * _bias_of_distance(PAGE_SIZE + tok - c, tab_ref, h), NEG_INF)
    d = tok - c
    o_ref[2] = jnp.where(same_head & (d >= 0), LOG2E * _bias_of_distance(jnp.maximum(d, 0), tab_ref, h), NEG_INF)


def _sample_bias_tiles(rel_bias, t_new):
    rows = 2 * t_new
    return pl.pallas_call(
        functools.partial(_sample_bias_body, t_new=t_new),
        grid=(H_B,),
        in_specs=[pl.BlockSpec(memory_space=pltpu.SMEM)],
        out_specs=pl.BlockSpec((3, rows, PAGE_SIZE * H_B), lambda h: (0, h, 0)),
        out_shape=jax.ShapeDtypeStruct((3, H_B * rows, PAGE_SIZE * H_B), F32),
        compiler_params=_params(("arbitrary",)),
        name="sample_bias",
    )(rel_bias.astype(F32).reshape(-1))


def _lambda(lam_ref, lam_init):
    lq = lam_ref[...]
    a = jnp.sum(lq[0:1] * lq[1:2], axis=1, keepdims=True)
    b = jnp.sum(lq[2:3] * lq[3:4], axis=1, keepdims=True)
    return jnp.exp(a) - jnp.exp(b) + lam_init


def _stack_maps(q):
    lane = lax.broadcasted_iota(jnp.int32, q.shape, 1)
    q = q * (DK_B ** -0.5 * LOG2E)
    return jnp.concatenate([jnp.where(lane < DK_B, q, 0.0), jnp.where(lane >= DK_B, q, 0.0)], axis=0)


def _attn_prompt_body(far_ref, q_ref, k_ref, vt_ref, bias_ref, lam_ref, sw_ref, o_ref,
                      qt_scr, m_scr, acc_scr, *, blk, hp, lam_init):
    h0 = pl.program_id(1) * hp
    i = pl.program_id(2)
    hd = 2 * DK_B
    for hh in range(hp):
        qt_scr[hh] = _stack_maps(q_ref[0, :, hh * hd:(hh + 1) * hd]).T.astype(BF16)
    m_scr[...] = jnp.full(m_scr.shape, NEG_INF, F32)
    acc_scr[...] = jnp.zeros(acc_scr.shape, F32)
    far = [far_ref[h0 + hh] * LOG2E for hh in range(hp)]
    gw = min(blk, 2 * LANES)
    groups = [(hh, slice(g * gw, (g + 1) * gw)) for hh in range(hp) for g in range(2 * blk // gw)]

    def bias_of(hh, cs, nkeys):
        c0 = cs.start % blk
        if nkeys == 2 * blk:
            return bias_ref[hh, :, :, c0:c0 + gw].reshape(2 * blk, gw)
        return bias_ref[hh, 1, :, c0:c0 + gw]

    def tile(start, nkeys, biased):
        kj = [k_ref[0, pl.ds(start, nkeys), hh * hd:(hh + 1) * hd] for hh in range(hp)]
        vtj = [vt_ref[0, hh, :, pl.ds(start, nkeys)] for hh in range(hp)]
        s = [jnp.dot(kj[hh], qt_scr[hh, :, cs], preferred_element_type=F32) for hh, cs in groups]
        if biased:
            s = [sg + bias_of(hh, cs, nkeys) for sg, (hh, cs) in zip(s, groups)]
        for sg, (hh, cs) in zip(s, groups):
            m_old = m_scr[hh, :, cs]
            m_cur = jnp.max(sg, axis=0, keepdims=True)
            if not biased:
                m_cur = m_cur + far[hh]
            mn = jnp.maximum(m_old, m_cur)
            p = jnp.exp2(sg - (mn if biased else mn - far[hh])).astype(BF16)
            pv = jnp.dot(vtj[hh], p, preferred_element_type=F32)
            acc_scr[hh, :, cs] = jnp.exp2(m_old - mn) * acc_scr[hh, :, cs] + pv
            m_scr[hh, :, cs] = mn

    n_far = jnp.maximum(i - 1, 0)

    def far_tile(j, carry):
        tile(pl.multiple_of(j * blk, blk), blk, False)
        return carry

    lax.fori_loop(0, n_far, far_tile, 0)

    @pl.when(i >= 1)
    def _():
        tile(pl.multiple_of((i - 1) * blk, blk), 2 * blk, True)

    @pl.when(i == 0)
    def _():
        tile(0, blk, True)

    lam = _lambda(lam_ref, lam_init)
    for hh in range(hp):
        acc = acc_scr[hh]
        wt = acc[:DV_B] / acc[DV_B:DV_B + 1]
        ot = wt[:, :blk] - lam * wt[:, blk:]
        ot = ot * lax.rsqrt(jnp.mean(ot * ot, axis=0, keepdims=True) + 1e-5)
        o_ref[0, :, hh * DV_B:(hh + 1) * DV_B] = (ot.T * sw_ref[...] * (1.0 - lam_init)).astype(o_ref.dtype)


def _attn_prompt(q, k16, vt16, rel_bias, lam_q, subln_w, lam_init, *, blk=512, hp=2):
    B, T, _ = q.shape
    blk = min(blk, T)
    bias = _prompt_bias_tiles(rel_bias, blk)
    far = rel_bias[NUM_BUCKETS - 1].astype(F32)
    vrows = vt16.shape[2]
    return pl.pallas_call(
        functools.partial(_attn_prompt_body, blk=blk, hp=hp, lam_init=lam_init),
        grid=(B, H_B // hp, T // blk),
        in_specs=[pl.BlockSpec(memory_space=pltpu.SMEM),
                  pl.BlockSpec((1, blk, hp * 2 * DK_B), lambda b, h, i: (b, i, h)),
                  pl.BlockSpec((1, T, hp * 2 * DK_B), lambda b, h, i: (b, 0, h)),
                  pl.BlockSpec((1, hp, vrows, T), lambda b, h, i: (b, h, 0, 0)),
                  pl.BlockSpec((hp, 2, blk, blk), lambda b, h, i: (h, 0, 0, 0)),
                  pl.BlockSpec((4, DK_B), lambda b, h, i: (0, 0)),
                  pl.BlockSpec((1, DV_B), lambda b, h, i: (0, 0))],
        out_specs=pl.BlockSpec((1, blk, hp * DV_B), lambda b, h, i: (b, i, h)),
        scratch_shapes=[pltpu.VMEM((hp, 2 * DK_B, 2 * blk), BF16), pltpu.VMEM((hp, 1, 2 * blk), F32),
                        pltpu.VMEM((hp, vrows, 2 * blk), F32)],
        out_shape=jax.ShapeDtypeStruct((B, T, H_B * DV_B), BF16),
        compiler_params=_params(("parallel", "parallel", "arbitrary")),
        name="attn_prompt",
    )(far, q, k16, vt16, bias, lam_q.astype(F32), subln_w.reshape(1, DV_B).astype(F32))


def _attn_sample_body(*refs, pp, t_new, lam_init):
    pt_ref, q_ref = refs[:2]
    k_refs = refs[2:2 + pp]
    v_refs = refs[2 + pp:2 + 2 * pp]
    kn_ref, vn_ref, bias_ref, lam_ref, sw_ref, o_ref, m_scr, l_scr, acc_scr = refs[2 + 2 * pp:]
    step = pl.program_id(1)
    last = step == pl.num_programs(1) - 1
    rows = 2 * t_new
    flat = PAGE_SIZE * H_B

    @pl.when(step == 0)
    def _():
        m_scr[...] = jnp.full(m_scr.shape, NEG_INF, F32)
        l_scr[...] = jnp.zeros(l_scr.shape, F32)
        acc_scr[...] = jnp.zeros(acc_scr.shape, F32)

    qall = jnp.concatenate([_stack_maps(q_ref[0, :, h * 2 * DK_B:(h + 1) * 2 * DK_B]) for h in range(H_B)],
                           axis=0).astype(BF16)

    def update(scores, values):
        m_old = m_scr[...]
        m_cur = jnp.max(scores[0], axis=1, keepdims=True)
        for s in scores[1:]:
            m_cur = jnp.maximum(m_cur, jnp.max(s, axis=1, keepdims=True))
        m_new = jnp.maximum(m_old, m_cur)
        alpha = jnp.exp2(m_old - m_new)
        l_new = alpha * l_scr[...]
        pv = None
        for s, v in zip(scores, values):
            p = jnp.exp2(s - m_new)
            l_new = l_new + jnp.sum(p, axis=1, keepdims=True)
            part = jnp.dot(p.astype(BF16), v, preferred_element_type=F32)
            pv = part if pv is None else pv + part
        l_scr[...] = l_new
        acc_scr[...] = alpha * acc_scr[...] + pv
        m_scr[...] = m_new

    def flat_page(ref):
        return ref[0].reshape(flat, 2 * DK_B).astype(BF16)

    scores = []
    for r in range(pp):
        bias = bias_ref[last.astype(jnp.int32)] if r == pp - 1 else bias_ref[0]
        scores.append(_dot_nt(qall, flat_page(k_refs[r])) + bias)
    update(scores, [flat_page(v_refs[r]) for r in range(pp)])

    @pl.when(last)
    def _():
        update([_dot_nt(qall, flat_page(kn_ref)) + bias_ref[2]], [flat_page(vn_ref)])
        w = acc_scr[...] / l_scr[...]
        lam = _lambda(lam_ref, lam_init)
        for h in range(H_B):
            o = w[h * rows:h * rows + t_new] - lam * w[h * rows + t_new:(h + 1) * rows]
            o = o * lax.rsqrt(jnp.mean(o * o, -1, keepdims=True) + 1e-5) * sw_ref[...] * (1.0 - lam_init)
            o_ref[0, :, h * DV_B:(h + 1) * DV_B] = o.astype(o_ref.dtype)


def _attn_sample(q, k_new, v_new, cache_k, cache_v, page_table, rel_bias, lam_q, subln_w, lam_init, *, pp=16):
    Bd, t_new, _ = q.shape
    n_pages = page_table.shape[1]
    assert n_pages % pp == 0
    bias = _sample_bias_tiles(rel_bias, t_new)
    padt = lambda a: jnp.pad(a, ((0, 0), (0, PAGE_SIZE - t_new), (0, 0), (0, 0)))
    page_spec = lambda r: pl.BlockSpec((1, PAGE_SIZE, H_B, 2 * DK_B),
                                       lambda b, s, pt, r=r: (pt[b, s * pp + r], 0, 0, 0))
    own_spec = pl.BlockSpec((1, PAGE_SIZE, H_B, 2 * DK_B), lambda b, s, pt: (b, 0, 0, 0))
    const = lambda *shape: pl.BlockSpec(shape, lambda b, s, pt: (0,) * len(shape))
    rows = H_B * 2 * t_new
    grid_spec = pltpu.PrefetchScalarGridSpec(
        num_scalar_prefetch=1,
        grid=(Bd, n_pages // pp),
        in_specs=([pl.BlockSpec((1, t_new, H_B * 2 * DK_B), lambda b, s, pt: (b, 0, 0))]
                  + [page_spec(r) for r in range(pp)] + [page_spec(r) for r in range(pp)]
                  + [own_spec, own_spec, const(3, rows, PAGE_SIZE * H_B), const(4, DK_B), const(1, DV_B)]),
        out_specs=pl.BlockSpec((1, t_new, H_B * DV_B), lambda b, s, pt: (b, 0, 0)),
        scratch_shapes=[pltpu.VMEM((rows, 1), F32), pltpu.VMEM((rows, 1), F32), pltpu.VMEM((rows, DV_B), F32)],
    )
    return pl.pallas_call(
        functools.partial(_attn_sample_body, pp=pp, t_new=t_new, lam_init=lam_init),
        grid_spec=grid_spec,
        out_shape=jax.ShapeDtypeStruct((Bd, t_new, H_B * DV_B), BF16),
        compiler_params=_params(("parallel", "arbitrary")),
        name="attn_sample",
    )(page_table, q, *([cache_k] * pp), *([cache_v] * pp), padt(k_new), padt(v_new), bias,
      lam_q.astype(F32), subln_w.reshape(1, DV_B).astype(F32))


def _gdn_mixer(x2d, B, T, w_qkv, w_zba, conv_w, a_log, dt_bias, gnorm_w, conv_state, ssm_state):
    state8 = jnp.pad(conv_state.astype(F32), ((0, 0), (8 - (CONV_W - 1), 0), (0, 0)))
    fused_tile = 512
    if T % fused_tile == 0:
        qkvc, zba2d, tail8 = _inproj_conv(x2d, w_qkv, w_zba, state8, conv_w.astype(F32), T, tm=fused_tile)
        qkvc = qkvc.reshape(B, T, CONV_DIM)
        xpad_tail = tail8[:, 8 - (CONV_W - 1):]
    else:
        proj = _mm(x2d, w_qkv).reshape(B, T, CONV_DIM)
        qkvc = _gdn_prep(proj, state8, conv_w.astype(F32), tt=fused_tile)
        xpad_tail = jnp.concatenate([conv_state.astype(F32), proj], axis=1)[:, -(CONV_W - 1):]
        zba2d = _mm(x2d, w_zba, tm=512)
    proj = zba2d.reshape(B, T, -1)
    bgc = _gdn_gates(zba2d, VDIM_A // LANES, a_log, dt_bias, min(CHUNK, T)).reshape(B, T, LANES)
    if T % CHUNK:
        tp = -(-T // CHUNK) * CHUNK
        padt = lambda a: jnp.pad(a, ((0, 0), (0, tp - T), (0, 0)))
        lane = jnp.arange(LANES)
        is_gc = (lane >= 2 * H_V) & (lane < 3 * H_V)
        tail = jnp.broadcast_to(jnp.where(is_gc, bgc[:, T - 1:T], 0.0), (B, tp - T, LANES))
        bgc = jnp.concatenate([bgc, tail], axis=1)
        qkvc = padt(qkvc)
        z_arr = padt(proj[:, :, :VDIM_A])
    else:
        z_arr = proj
    u, w, qg, kd, aqk, dec = _gdn_intra(qkvc, bgc)
    y, s_new = _gdn_state(u, w, qg, kd, aqk, dec, z_arr, 0, ssm_state, gnorm_w)
    return y[:, :T].reshape(B * T, VDIM_A), xpad_tail, s_new


def kernel(x_prompt, x_sample, state_conv, state_ssm, cache_k, cache_v, page_table, p_prompt, p_sample,
           w_in_a, conv_w_a, a_log_a, dt_bias_a, gnorm_w_a, w_out_a, w_kv, w_q_b, lam_b, subln_w_b, w_o_b,
           rel_bias, ln_g, ln_b, w_up, w_down, w_ple, w_pg):
    bf = lambda w: w.astype(BF16)
    w_qkv = bf(w_in_a[:, :, :CONV_DIM])
    w_zba = bf(jnp.pad(w_in_a[:, :, CONV_DIM:], ((0, 0), (0, 0), (0, IN_A_PAD - IN_A))))
    w_out, w_kv16, w_q, w_o = bf(w_out_a), bf(w_kv), bf(w_q_b), bf(w_o_b)
    w_up16, w_down16, w_ple16, w_pg16 = bf(w_up), bf(w_down), bf(w_ple), bf(w_pg)

    def run(x3d, p, conv_state, ssm_state, attn_fn):
        B, T, _ = x3d.shape
        x = x3d.reshape(B * T, D_MODEL)
        convs, ssms = [], []
        k_sh = v_sh = kv16 = None
        for i in range(DEPTH):
            if i < N_A:
                y, cs, ss = _gdn_mixer(x, B, T, w_qkv[i], w_zba[i], conv_w_a[i], a_log_a[i], dt_bias_a[i],
                                       gnorm_w_a[i], conv_state[i], ssm_state[i])
                convs.append(cs)
                ssms.append(ss)
                w_mix = w_out[i]
            else:
                j = i - N_A
                lam_init = 0.8 - 0.6 * math.exp(-0.3 * i)
                q = _mm(x, w_q[j]).reshape(B, T, H_B * 2 * DK_B)
                y = attn_fn(q, k_sh, v_sh, kv16, lam_b[j], subln_w_b[j], lam_init).reshape(B * T, H_B * DV_B)
                w_mix = w_o[j]
            x = _layer_tail(x, y, p[i].reshape(B * T, -1), w_mix, ln_g[i, 0], ln_b[i, 0], w_up16[i], w_down16[i],
                            ln_g[i, 1], ln_b[i, 1], w_ple16[i], w_pg16[i])
            if i == N_A - 1:
                k_sh, v_sh, k16, vt16 = _kv_proj(x, w_kv16, T)
                k_sh = k_sh.reshape(B, T, H_B, 2 * DK_B)
                v_sh = v_sh.reshape(B, T, H_B, DV_B)
                kv16 = (k16.reshape(B, T, -1), vt16)
        return x.reshape(B, T, D_MODEL), jnp.stack(convs), jnp.stack(ssms), k_sh, v_sh

    def attn_prompt(q, k_sh, v_sh, kv16, lam_q, sw, lam_init):
        return _attn_prompt(q, kv16[0], kv16[1], rel_bias, lam_q, sw, lam_init)

    def attn_sample(q, k_sh, v_sh, kv16, lam_q, sw, lam_init):
        return _attn_sample(q, k_sh, v_sh, cache_k, cache_v, page_table, rel_bias, lam_q, sw, lam_init)

    Bp = x_prompt.shape[0]
    conv0 = jnp.zeros((N_A, Bp, CONV_W - 1, CONV_DIM), x_prompt.dtype)
    ssm0 = jnp.zeros((N_A, Bp, H_V, DK_A, DV_A), state_ssm.dtype)
    y_prompt, conv_p, ssm_p, k_p, v_p = run(x_prompt, p_prompt, conv0, ssm0, attn_prompt)
    y_sample, conv_s, ssm_s, k_s, v_s = run(x_sample, p_sample, state_conv, state_ssm, attn_sample)
    return (y_prompt, y_sample, conv_p, ssm_p, k_p, v_p, conv_s, ssm_s, k_s, v_s)
```
